```python
import math
import jax, jax.numpy as jnp
from jax import lax
import numpy as np

D_MODEL = 4096
BATCH = 2
SEQ = 4096
DEPTH = 2

N_A_LAYERS = DEPTH // 2
N_B_LAYERS = DEPTH - N_A_LAYERS

SSM_WIDTH = D_MODEL // 2
SSM_GROUP = 16
SSM_GROUPS = SSM_WIDTH // SSM_GROUP
SSM_STATE = 64
DT_MIN = 1e-3
DT_MAX = 1e-1

N_HEADS = 16
HEAD_DIM = D_MODEL // (2 * N_HEADS)
ATTN_WIDTH = N_HEADS * 2 * HEAD_DIM
Q_BLOCK = 128

N_GROUPS = 4
EXPERTS_PER_GROUP = 4
N_EXPERTS = N_GROUPS * EXPERTS_PER_GROUP
TOP_K_INNER = 2
EXPERT_FF = D_MODEL // 4

EPS = 1e-6

kernel_name = "yoco_s5_diffattn_hmoe"


def rmsnorm(x, g):
    xf = x.astype(jnp.float32)
    y = xf * lax.rsqrt(jnp.mean(xf * xf, axis=-1, keepdims=True) + EPS)
    return (y * g.astype(jnp.float32)).astype(x.dtype)


def s5_mixer(xn, w_in, a_re, a_im, log_dt, b_re, b_im, c_re, c_im, d_skip, w_glu_a, w_glu_b):
    f32 = jnp.float32
    bsz, seq, _ = xn.shape
    u = (xn @ w_in).astype(f32).reshape(bsz, seq, SSM_GROUPS, SSM_GROUP)
    lam_re = a_re.astype(f32)
    lam_im = a_im.astype(f32)
    dt = jnp.exp(log_dt.astype(f32))[:, None]
    mag = jnp.exp(lam_re * dt)
    lb_re = mag * jnp.cos(lam_im * dt)
    lb_im = mag * jnp.sin(lam_im * dt)
    den = lam_re * lam_re + lam_im * lam_im
    nr = lb_re - 1.0
    q_re = (nr * lam_re + lb_im * lam_im) / den
    q_im = (lb_im * lam_re - nr * lam_im) / den
    br = b_re.astype(f32)
    bi = b_im.astype(f32)
    bb_re = q_re[..., None] * br - q_im[..., None] * bi
    bb_im = q_re[..., None] * bi + q_im[..., None] * br
    bu_re = jnp.einsum('blgj,gpj->blgp', u, bb_re)
    bu_im = jnp.einsum('blgj,gpj->blgp', u, bb_im)
    a_re_t = jnp.broadcast_to(lb_re, bu_re.shape)
    a_im_t = jnp.broadcast_to(lb_im, bu_im.shape)

    def combine(e1, e2):
        a1r, a1i, b1r, b1i = e1
        a2r, a2i, b2r, b2i = e2
        return (a2r * a1r - a2i * a1i,
                a2r * a1i + a2i * a1r,
                a2r * b1r - a2i * b1i + b2r,
                a2r * b1i + a2i * b1r + b2i)

    _, _, s_re, s_im = lax.associative_scan(combine, (a_re_t, a_im_t, bu_re, bu_im), axis=1)
    y = (jnp.einsum('blgp,gjp->blgj', s_re, c_re.astype(f32))
         - jnp.einsum('blgp,gjp->blgj', s_im, c_im.astype(f32))
         + d_skip.astype(f32) * u)
    y = jax.nn.gelu(y.reshape(bsz, seq, SSM_WIDTH)).astype(xn.dtype)
    return (y @ w_glu_a) * jax.nn.sigmoid(y @ w_glu_b)


def diff_attention(xn, k, v, w_q, lq1, lk1, lq2, lk2, subln_g, w_o, lambda_init):
    f32 = jnp.float32
    bsz, seq, _ = xn.shape
    n_blocks = seq // Q_BLOCK
    q = (xn @ w_q).reshape(bsz, n_blocks, Q_BLOCK, N_HEADS, 2, HEAD_DIM)
    q = jnp.moveaxis(q, 1, 0)
    lam = (jnp.exp(jnp.sum(lq1.astype(f32) * lk1.astype(f32)))
           - jnp.exp(jnp.sum(lq2.astype(f32) * lk2.astype(f32))) + lambda_init)
    scale = HEAD_DIM ** -0.5
    slopes = 2.0 ** (-8.0 * jnp.arange(1, N_HEADS + 1, dtype=f32) / N_HEADS)
    kpos = jnp.arange(seq, dtype=jnp.int32)

    def block(args):
        qblk, bi = args
        qpos = bi * Q_BLOCK + jnp.arange(Q_BLOCK, dtype=jnp.int32)
        dist = (qpos[:, None] - kpos[None, :]).astype(f32)
        bias = jnp.where(dist >= 0, -slopes[:, None, None] * dist, -jnp.inf)
        s = jnp.einsum('bqhcd,bkhcd->bchqk', qblk, k).astype(f32) * scale + bias
        p = jax.nn.softmax(s, axis=-1)
        w = (p[:, 0] - lam * p[:, 1]).astype(v.dtype)
        return jnp.einsum('bhqk,bkhe->bqhe', w, v)

    o = lax.map(block, (q, jnp.arange(n_blocks, dtype=jnp.int32)))
    o = jnp.moveaxis(o, 0, 1).reshape(bsz, seq, N_HEADS, 2 * HEAD_DIM)
    o = rmsnorm(o, subln_g) * (1.0 - lambda_init)
    return o.reshape(bsz, seq, ATTN_WIDTH) @ w_o


def hier_moe(xn, w_group, b_group, w_router, b_router, w_gate, w_up, w_down):
    f32 = jnp.float32
    bsz, seq, d = xn.shape
    t = xn.reshape(-1, d)
    g_logits = (t @ w_group).astype(f32) + b_group.astype(f32)
    g_prob = jax.nn.softmax(g_logits, axis=-1)
    g_val, g_idx = lax.top_k(g_prob, 1)
    e_logits = ((t @ w_router).astype(f32) + b_router.astype(f32)).reshape(-1, N_GROUPS, EXPERTS_PER_GROUP)
    e_sel = jnp.take_along_axis(e_logits, g_idx[:, :, None], axis=1)[:, 0]
    top_v, top_i = lax.top_k(e_sel, TOP_K_INNER)
    top_w = jax.nn.softmax(top_v, axis=-1) * g_val
    expert_id = g_idx * EXPERTS_PER_GROUP + top_i
    gates = jnp.sum(jax.nn.one_hot(expert_id, N_EXPERTS, dtype=f32) * top_w[..., None], axis=1)
    h = jax.nn.silu(jnp.einsum('td,edf->tef', t, w_gate)) * jnp.einsum('td,edf->tef', t, w_up)
    y = jnp.einsum('tef,efd->td', h * gates[:, :, None].astype(h.dtype), w_down)
    return y.reshape(bsz, seq, d)


def setup_inputs(seed: int = 0) -> dict:
    key = jax.random.key(seed)
    ks = iter(jax.random.split(key, 40))
    f32 = jnp.float32

    def nrm(shape, scale):
        return jax.random.normal(next(ks), shape, f32) * scale

    def gain(shape):
        return jnp.ones(shape, f32) + nrm(shape, 0.01)

    na, nb = N_A_LAYERS, N_B_LAYERS
    n_idx = jnp.arange(SSM_STATE, dtype=f32)
    return {
        "x": nrm((BATCH, SEQ, D_MODEL), 1.0),
        "ssm_w_in": nrm((na, D_MODEL, SSM_WIDTH), D_MODEL ** -0.5),
        "ssm_a_re": -0.5 + nrm((na, SSM_GROUPS, SSM_STATE), 0.01),
        "ssm_a_im": math.pi * n_idx + nrm((na, SSM_GROUPS, SSM_STATE), 0.01),
        "ssm_log_dt": jax.random.uniform(next(ks), (na, SSM_GROUPS), f32, math.log(DT_MIN), math.log(DT_MAX)),
        "ssm_b_re": nrm((na, SSM_GROUPS, SSM_STATE, SSM_GROUP), SSM_GROUP ** -0.5),
        "ssm_b_im": nrm((na, SSM_GROUPS, SSM_STATE, SSM_GROUP), SSM_GROUP ** -0.5),
        "ssm_c_re": nrm((na, SSM_GROUPS, SSM_GROUP, SSM_STATE), SSM_STATE ** -0.5),
        "ssm_c_im": nrm((na, SSM_GROUPS, SSM_GROUP, SSM_STATE), SSM_STATE ** -0.5),
        "ssm_d": nrm((na, SSM_GROUPS, SSM_GROUP), 1.0),
        "ssm_w_glu_a": nrm((na, SSM_WIDTH, D_MODEL), SSM_WIDTH ** -0.5),
        "ssm_w_glu_b": nrm((na, SSM_WIDTH, D_MODEL), SSM_WIDTH ** -0.5),
        "kv_norm_g": gain((D_MODEL,)),
        "w_k": nrm((D_MODEL, ATTN_WIDTH), D_MODEL ** -0.5),
        "w_v": nrm((D_MODEL, ATTN_WIDTH), D_MODEL ** -0.5),
        "attn_w_q": nrm((nb, D_MODEL, ATTN_WIDTH), D_MODEL ** -0.5),
        "attn_lambda_q1": nrm((nb, HEAD_DIM), 0.1),
        "attn_lambda_k1": nrm((nb, HEAD_DIM), 0.1),
        "attn_lambda_q2": nrm((nb, HEAD_DIM), 0.1),
        "attn_lambda_k2": nrm((nb, HEAD_DIM), 0.1),
        "attn_subln_g": gain((nb, 2 * HEAD_DIM)),
        "attn_w_o": nrm((nb, ATTN_WIDTH, D_MODEL), ATTN_WIDTH ** -0.5),
        "moe_w_group": nrm((DEPTH, D_MODEL, N_GROUPS), D_MODEL ** -0.5),
        "moe_b_group": nrm((DEPTH, N_GROUPS), 0.01),
        "moe_w_router": nrm((DEPTH, D_MODEL, N_EXPERTS), D_MODEL ** -0.5),
        "moe_b_router": nrm((DEPTH, N_EXPERTS), 0.01),
        "moe_w_gate": nrm((DEPTH, N_EXPERTS, D_MODEL, EXPERT_FF), D_MODEL ** -0.5),
        "moe_w_up": nrm((DEPTH, N_EXPERTS, D_MODEL, EXPERT_FF), D_MODEL ** -0.5),
        "moe_w_down": nrm((DEPTH, N_EXPERTS, EXPERT_FF, D_MODEL), EXPERT_FF ** -0.5),
        "norm_mix_g": gain((DEPTH, D_MODEL)),
        "norm_ffn_g": gain((DEPTH, D_MODEL)),
        "final_norm_g": gain((D_MODEL,)),
    }


def reference(x, ssm_w_in, ssm_a_re, ssm_a_im, ssm_log_dt, ssm_b_re, ssm_b_im, ssm_c_re, ssm_c_im,
              ssm_d, ssm_w_glu_a, ssm_w_glu_b, kv_norm_g, w_k, w_v, attn_w_q, attn_lambda_q1,
              attn_lambda_k1, attn_lambda_q2, attn_lambda_k2, attn_subln_g, attn_w_o, moe_w_group,
              moe_b_group, moe_w_router, moe_b_router, moe_w_gate, moe_w_up, moe_w_down,
              norm_mix_g, norm_ffn_g, final_norm_g):
    bsz, seq, _ = x.shape
    h = x
    k_shared = None
    v_shared = None
    for li in range(DEPTH):
        hn = rmsnorm(h, norm_mix_g[li])
        if li < N_A_LAYERS:
            a = li
            h = h + s5_mixer(hn, ssm_w_in[a], ssm_a_re[a], ssm_a_im[a], ssm_log_dt[a], ssm_b_re[a],
                             ssm_b_im[a], ssm_c_re[a], ssm_c_im[a], ssm_d[a], ssm_w_glu_a[a], ssm_w_glu_b[a])
        else:
            b = li - N_A_LAYERS
            lambda_init = 0.8 - 0.6 * math.exp(-0.3 * li)
            h = h + diff_attention(hn, k_shared, v_shared, attn_w_q[b], attn_lambda_q1[b], attn_lambda_k1[b],
                                   attn_lambda_q2[b], attn_lambda_k2[b], attn_subln_g[b], attn_w_o[b], lambda_init)
        h = h + hier_moe(rmsnorm(h, norm_ffn_g[li]), moe_w_group[li], moe_b_group[li], moe_w_router[li],
                         moe_b_router[li], moe_w_gate[li], moe_w_up[li], moe_w_down[li])
        if li == N_A_LAYERS - 1:
            kvn = rmsnorm(h, kv_norm_g)
            k_shared = (kvn @ w_k).reshape(bsz, seq, N_HEADS, 2, HEAD_DIM)
            v_shared = (kvn @ w_v).reshape(bsz, seq, N_HEADS, 2 * HEAD_DIM)
    return rmsnorm(h, final_norm_g)
```

```python
import functools
import math

import jax
import jax.numpy as jnp
from jax import lax
from jax.experimental import pallas as pl
from jax.experimental.pallas import tpu as pltpu

EPS = 1e-6
F32 = jnp.float32
BF16 = jnp.bfloat16

V7X_VMEM_BYTES = 64 * 1024 * 1024
VMEM_LIMIT = V7X_VMEM_BYTES - 8 * 1024 * 1024
LANES = 128

SSM_CHUNK = 16
N_GROUPS = 4
EXPERTS_PER_GROUP = 4
ROUTE_LANES = LANES


def _cparams(sem, vmem=VMEM_LIMIT):
    return pltpu.CompilerParams(dimension_semantics=sem, vmem_limit_bytes=vmem)


def _rms(x, g):
    inv = lax.rsqrt(jnp.mean(x * x, axis=-1, keepdims=True) + EPS)
    return x * inv * g


def _rmsnorm_kernel(x_ref, g_ref, o_ref):
    o_ref[...] = _rms(x_ref[...], g_ref[...]).astype(o_ref.dtype)


def _rmsnorm(x, g, out_dtype, tm=256):
    t, d = x.shape
    tm = min(tm, t)
    return pl.pallas_call(
        _rmsnorm_kernel,
        grid=(t // tm,),
        in_specs=[pl.BlockSpec((tm, d), lambda i: (i, 0)), pl.BlockSpec((1, d), lambda i: (0, 0))],
        out_specs=pl.BlockSpec((tm, d), lambda i: (i, 0)),
        out_shape=jax.ShapeDtypeStruct((t, d), out_dtype),
        compiler_params=_cparams(("arbitrary",)),
        name="rmsnorm",
    )(x, g.reshape(1, d))


def _mm_kernel(x_ref, w_ref, o_ref, wb_ref, *, scale):
    @pl.when(pl.program_id(1) == 0)
    def _():
        wb_ref[...] = w_ref[...].astype(BF16)

    acc = jnp.dot(x_ref[...], wb_ref[...], preferred_element_type=F32)
    if scale != 1.0:
        acc = acc * scale
    o_ref[...] = acc.astype(o_ref.dtype)


def _mm_res_kernel(x_ref, w_ref, r_ref, o_ref, wb_ref):
    @pl.when(pl.program_id(1) == 0)
    def _():
        wb_ref[...] = w_ref[...].astype(BF16)

    o_ref[...] = r_ref[...] + jnp.dot(x_ref[...], wb_ref[...], preferred_element_type=F32)


def _glu_res_kernel(x_ref, wa_ref, wb_ref, r_ref, o_ref, wab_ref, wbb_ref):
    @pl.when(pl.program_id(1) == 0)
    def _():
        wab_ref[...] = wa_ref[...].astype(BF16)
        wbb_ref[...] = wb_ref[...].astype(BF16)

    x = x_ref[...]
    a = jnp.dot(x, wab_ref[...], preferred_element_type=F32)
    b = jnp.dot(x, wbb_ref[...], preferred_element_type=F32)
    o_ref[...] = r_ref[...] + a * jax.nn.sigmoid(b)


def _mm_tiles(m, n, tm, tn):
    return min(tm, m), min(tn, n)


def _matmul(x, w, out_dtype, scale=1.0, tm=1024, tn=512):
    m, k = x.shape
    n = w.shape[1]
    tm, tn = _mm_tiles(m, n, tm, tn)
    return pl.pallas_call(
        functools.partial(_mm_kernel, scale=scale),
        grid=(n // tn, m // tm),
        in_specs=[pl.BlockSpec((tm, k), lambda j, i: (i, 0)), pl.BlockSpec((k, tn), lambda j, i: (0, j))],
        out_specs=pl.BlockSpec((tm, tn), lambda j, i: (i, j)),
        out_shape=jax.ShapeDtypeStruct((m, n), out_dtype),
        scratch_shapes=[pltpu.VMEM((k, tn), BF16)],
        compiler_params=_cparams(("arbitrary", "arbitrary")),
        name="matmul",
    )(x, w)


def _matmul_res(x, w, res, tm=1024, tn=512):
    m, k = x.shape
    n = w.shape[1]
    tm, tn = _mm_tiles(m, n, tm, tn)
    return pl.pallas_call(
        _mm_res_kernel,
        grid=(n // tn, m // tm),
        in_specs=[pl.BlockSpec((tm, k), lambda j, i: (i, 0)), pl.BlockSpec((k, tn), lambda j, i: (0, j)),
                  pl.BlockSpec((tm, tn), lambda j, i: (i, j))],
        out_specs=pl.BlockSpec((tm, tn), lambda j, i: (i, j)),
        out_shape=jax.ShapeDtypeStruct((m, n), F32),
        scratch_shapes=[pltpu.VMEM((k, tn), BF16)],
        compiler_params=_cparams(("arbitrary", "arbitrary")),
        name="matmul_res",
    )(x, w, res)


def _glu_res(x, wa, wb, res, tm=1024, tn=512):
    m, k = x.shape
    n = wa.shape[1]
    tm, tn = _mm_tiles(m, n, tm, tn)
    wspec = pl.BlockSpec((k, tn), lambda j, i: (0, j))
    return pl.pallas_call(
        _glu_res_kernel,
        grid=(n // tn, m // tm),
        in_specs=[pl.BlockSpec((tm, k), lambda j, i: (i, 0)), wspec, wspec,
                  pl.BlockSpec((tm, tn), lambda j, i: (i, j))],
        out_specs=pl.BlockSpec((tm, tn), lambda j, i: (i, j)),
        out_shape=jax.ShapeDtypeStruct((m, n), F32),
        scratch_shapes=[pltpu.VMEM((k, tn), BF16), pltpu.VMEM((k, tn), BF16)],
        compiler_params=_cparams(("arbitrary", "arbitrary")),
        name="glu_res",
    )(x, wa, wb, res)


def _ssm_tables(a_re, a_im, log_dt, b_re, b_im, c_re, c_im, d_skip, c):
    hp = lax.Precision.HIGHEST
    g_n, p_n = a_re.shape
    j_n = d_skip.shape[1]
    dt = jnp.exp(log_dt)[:, None]
    steps = jnp.arange(c + 1, dtype=F32)[:, None, None]
    mag = jnp.exp(steps * (a_re * dt)[None])
    ang = steps * (a_im * dt)[None]
    pw_re = mag * jnp.cos(ang)
    pw_im = mag * jnp.sin(ang)
    lb_re, lb_im = pw_re[1], pw_im[1]
    den = a_re * a_re + a_im * a_im
    nr = lb_re - 1.0
    q_re = (nr * a_re + lb_im * a_im) / den
    q_im = (lb_im * a_re - nr * a_im) / den
    bb_re = q_re[..., None] * b_re - q_im[..., None] * b_im
    bb_im = q_re[..., None] * b_im + q_im[..., None] * b_re
    cp_re = c_re[None] * pw_re[:, :, None, :] - c_im[None] * pw_im[:, :, None, :]
    cp_im = c_re[None] * pw_im[:, :, None, :] + c_im[None] * pw_re[:, :, None, :]
    kern = (jnp.einsum('dgjp,gpk->gdkj', cp_re[:c], bb_re, precision=hp)
            - jnp.einsum('dgjp,gpk->gdkj', cp_im[:c], bb_im, precision=hp))
    lag = jnp.arange(c)[None, :] - jnp.arange(c)[:, None]
    m_intra = jnp.where((lag >= 0)[None, :, :, None, None], kern[:, jnp.clip(lag, 0, c - 1)], 0.0)
    m_intra = m_intra.transpose(0, 1, 3, 2, 4).reshape(g_n, c * j_n, c * j_n)
    m_intra = m_intra + jax.vmap(jnp.diag)(jnp.tile(d_skip, (1, c)))
    rp_re = pw_re[:c][::-1]
    rp_im = pw_im[:c][::-1]
    bbt_re = bb_re.transpose(0, 2, 1)
    bbt_im = bb_im.transpose(0, 2, 1)
    in_re = rp_re.transpose(1, 0, 2)[:, :, None, :] * bbt_re[:, None] - rp_im.transpose(1, 0, 2)[:, :, None, :] * bbt_im[:, None]
    in_im = rp_re.transpose(1, 0, 2)[:, :, None, :] * bbt_im[:, None] + rp_im.transpose(1, 0, 2)[:, :, None, :] * bbt_re[:, None]
    in_re = in_re.reshape(g_n, c * j_n, p_n)
    in_im = in_im.reshape(g_n, c * j_n, p_n)
    out_re = cp_re[1:].transpose(1, 3, 0, 2).reshape(g_n, p_n, c * j_n)
    out_im = -cp_im[1:].transpose(1, 3, 0, 2).reshape(g_n, p_n, c * j_n)

    def pair_cols(m):
        r, w = m.shape[1], m.shape[2]
        m = m.reshape(g_n // 2, 2, r, w)
        z = jnp.zeros_like(m[:, 0])
        top = jnp.concatenate([m[:, 0], z], axis=-1)
        bot = jnp.concatenate([z, m[:, 1]], axis=-1)
        return jnp.concatenate([top, bot], axis=1)

    return dict(
        m_intra=m_intra.astype(BF16),
        in_re=pair_cols(in_re).astype(BF16), in_im=pair_cols(in_im).astype(BF16),
        out_re=pair_cols(out_re).astype(BF16), out_im=pair_cols(out_im).astype(BF16),
        lam_re=pw_re[c].reshape(g_n // 2, 2 * p_n), lam_im=pw_im[c].reshape(g_n // 2, 2 * p_n),
    )


def _ssm_in_kernel(u_ref, wre_ref, wim_ref, xre_ref, xim_ref):
    up = jnp.concatenate([u_ref[0], u_ref[1]], axis=-1)
    xre_ref[...] = jnp.dot(up, wre_ref[0], preferred_element_type=F32).reshape(xre_ref.shape)
    xim_ref[...] = jnp.dot(up, wim_ref[0], preferred_element_type=F32).reshape(xim_ref.shape)


def _ssm_scan_kernel(xre_ref, xim_ref, lre_ref, lim_ref, ore_ref, oim_ref, sre_ref, sim_ref, *, nb):
    @pl.when(pl.program_id(0) == 0)
    def _():
        sre_ref[...] = jnp.zeros_like(sre_ref)
        sim_ref[...] = jnp.zeros_like(sim_ref)

    lre = lre_ref[...]
    lim = lim_ref[...]
    s_re = sre_ref[...]
    s_im = sim_ref[...]
    for n in range(nb):
        ore_ref[:, n] = s_re
        oim_ref[:, n] = s_im
        x_re = xre_ref[:, n]
        x_im = xim_ref[:, n]
        s_re, s_im = lre * s_re - lim * s_im + x_re, lre * s_im + lim * s_re + x_im
    sre_ref[...] = s_re
    sim_ref[...] = s_im


def _ssm_out_kernel(u_ref, mi_ref, sre_ref, sim_ref, more_ref, moim_ref, y_ref):
    m = u_ref.shape[1]
    w = u_ref.shape[2]
    sre = sre_ref[...].reshape(m, sre_ref.shape[-1]).astype(BF16)
    sim = sim_ref[...].reshape(m, sim_ref.shape[-1]).astype(BF16)
    y_state = (jnp.dot(sre, more_ref[0], preferred_element_type=F32)
               + jnp.dot(sim, moim_ref[0], preferred_element_type=F32))
    for t in range(2):
        y = jnp.dot(u_ref[t], mi_ref[t], preferred_element_type=F32) + y_state[:, t * w:(t + 1) * w]
        y_ref[t] = jax.nn.gelu(y).astype(y_ref.dtype)


def _s5_core(u, bsz, seq, tabs, c, scan_block=32):
    t, width = u.shape
    g_n = tabs["m_intra"].shape[0]
    j_n = width // g_n
    q_n = g_n // 2
    n_chunks = seq // c
    m = bsz * n_chunks
    cj = c * j_n
    p2 = tabs["lam_re"].shape[1]
    u_t = u.reshape(bsz, n_chunks, c, g_n, j_n).transpose(3, 0, 1, 2, 4).reshape(g_n, m, cj)

    u_spec = pl.BlockSpec((2, m, cj), lambda q: (q, 0, 0))
    x_spec = pl.BlockSpec((bsz, n_chunks, p2), lambda q: (0, 0, q))
    x_shape = jax.ShapeDtypeStruct((bsz, n_chunks, q_n * p2), F32)
    xre, xim = pl.pallas_call(
        _ssm_in_kernel,
        grid=(q_n,),
        in_specs=[u_spec, pl.BlockSpec((1, 2 * cj, p2), lambda q: (q, 0, 0)),
                  pl.BlockSpec((1, 2 * cj, p2), lambda q: (q, 0, 0))],
        out_specs=[x_spec, x_spec],
        out_shape=[x_shape, x_shape],
        compiler_params=_cparams(("arbitrary",)),
        name="ssm_chunk_in",
    )(u_t, tabs["in_re"], tabs["in_im"])

    nb = min(scan_block, n_chunks)
    s_spec = pl.BlockSpec((bsz, nb, q_n, p2), lambda i: (0, i, 0, 0))
    l_spec = pl.BlockSpec((q_n, p2), lambda i: (0, 0))
    s_shape = jax.ShapeDtypeStruct((bsz, n_chunks, q_n, p2), F32)
    sre, sim = pl.pallas_call(
        functools.partial(_ssm_scan_kernel, nb=nb),
        grid=(n_chunks // nb,),
        in_specs=[s_spec, s_spec, l_spec, l_spec],
        out_specs=[s_spec, s_spec],
        out_shape=[s_shape, s_shape],
        scratch_shapes=[pltpu.VMEM((bsz, q_n, p2), F32), pltpu.VMEM((bsz, q_n, p2), F32)],
        compiler_params=_cparams(("arbitrary",)),
        name="ssm_chunk_scan",
    )(xre.reshape(bsz, n_chunks, q_n, p2), xim.reshape(bsz, n_chunks, q_n, p2), tabs["lam_re"], tabs["lam_im"])

    y_t = pl.pallas_call(
        _ssm_out_kernel,
        grid=(q_n,),
        in_specs=[u_spec, pl.BlockSpec((2, cj, cj), lambda q: (q, 0, 0)), x_spec, x_spec,
                  pl.BlockSpec((1, p2, 2 * cj), lambda q: (q, 0, 0)),
                  pl.BlockSpec((1, p2, 2 * cj), lambda q: (q, 0, 0))],
        out_specs=u_spec,
        out_shape=jax.ShapeDtypeStruct((g_n, m, cj), BF16),
        compiler_params=_cparams(("arbitrary",)),
        name="ssm_chunk_out",
    )(u_t, tabs["m_intra"], sre.reshape(bsz, n_chunks, q_n * p2), sim.reshape(bsz, n_chunks, q_n * p2),
      tabs["out_re"], tabs["out_im"])
    return y_t.reshape(g_n, bsz, n_chunks, c, j_n).transpose(1, 2, 3, 0, 4).reshape(t, width)


def _attn_kernel(lam_ref, slope_ref, q_ref, k_ref, v_ref, g_ref, o_ref, acc0_ref, acc1_ref, *, tq, post_scale):
    h = pl.program_id(1)
    qi = pl.program_id(2)
    slope = slope_ref[h]
    lam = lam_ref[0]
    dh = q_ref.shape[1] // 2
    q0 = q_ref[:, :dh]
    q1 = q_ref[:, dh:]
    col = lax.broadcasted_iota(jnp.int32, (1, tq), 1).astype(F32)
    nt = (((1,), (1,)), ((), ()))

    def scores(j):
        ks = pl.multiple_of(j * tq, tq)
        kt = k_ref[pl.ds(ks, tq), :]
        vt = v_ref[pl.ds(ks, tq), :]
        kb = slope * (col + ((j - qi) * tq).astype(F32))
        s0 = lax.dot_general(q0, kt[:, :dh], nt, preferred_element_type=F32) + kb
        s1 = lax.dot_general(q1, kt[:, dh:], nt, preferred_element_type=F32) + kb
        return s0, s1, vt

    def update(s, m, l, acc_ref, vt):
        m_new = jnp.maximum(m, jnp.max(s, axis=-1, keepdims=True))
        alpha = jnp.exp(m - m_new)
        p = jnp.exp(s - m_new)
        l_new = alpha * l + jnp.sum(p, axis=-1, keepdims=True)
        acc_ref[...] = alpha * acc_ref[...] + jnp.dot(p.astype(BF16), vt, preferred_element_type=F32)
        return m_new, l_new

    acc0_ref[...] = jnp.zeros_like(acc0_ref)
    acc1_ref[...] = jnp.zeros_like(acc1_ref)

    def body(j, carry):
        m0, l0, m1, l1 = carry
        s0, s1, vt = scores(j)
        m0, l0 = update(s0, m0, l0, acc0_ref, vt)
        m1, l1 = update(s1, m1, l1, acc1_ref, vt)
        return m0, l0, m1, l1

    neg = jnp.full((tq, 1), -1e30, F32)
    zero = jnp.zeros((tq, 1), F32)
    m0, l0, m1, l1 = lax.fori_loop(0, qi, body, (neg, zero, neg, zero))

    s0, s1, vt = scores(qi)
    causal = lax.broadcasted_iota(jnp.int32, (tq, tq), 0) >= lax.broadcasted_iota(jnp.int32, (tq, tq), 1)
    m0, l0 = update(jnp.where(causal, s0, -jnp.inf), m0, l0, acc0_ref, vt)
    m1, l1 = update(jnp.where(causal, s1, -jnp.inf), m1, l1, acc1_ref, vt)

    o = acc0_ref[...] / l0 - lam * (acc1_ref[...] / l1)
    o_ref[...] = (_rms(o, g_ref[...]) * post_scale).astype(o_ref.dtype)


def _diff_attention(q, k, v, lam, slopes, subln_g, bsz, seq, n_heads, post_scale, tq=512):
    t, width = q.shape
    hw = width // n_heads
    tq = min(tq, seq)
    nq = seq // tq
    return pl.pallas_call(
        functools.partial(_attn_kernel, tq=tq, post_scale=post_scale),
        grid=(bsz, n_heads, nq),
        in_specs=[pl.BlockSpec(memory_space=pltpu.SMEM), pl.BlockSpec(memory_space=pltpu.SMEM),
                  pl.BlockSpec((tq, hw), lambda b, h, i: (b * nq + i, h)),
                  pl.BlockSpec((seq, hw), lambda b, h, i: (b, h)),
                  pl.BlockSpec((seq, hw), lambda b, h, i: (b, h)),
                  pl.BlockSpec((1, hw), lambda b, h, i: (0, 0))],
        out_specs=pl.BlockSpec((tq, hw), lambda b, h, i: (b * nq + i, h)),
        out_shape=jax.ShapeDtypeStruct((t, width), BF16),
        scratch_shapes=[pltpu.VMEM((tq, hw), F32), pltpu.VMEM((tq, hw), F32)],
        compiler_params=_cparams(("arbitrary", "arbitrary", "arbitrary")),
        name="diff_attention",
    )(lam, slopes, q, k, v, subln_g.reshape(1, hw))


def _split_bf16(x):
    hi = x.astype(BF16)
    lo = (x - hi.astype(F32)).astype(BF16)
    return hi, lo


def _router_kernel(x_ref, g_ref, w_ref, b_ref, o_ref, *, n_groups, per_group):
    xn = _rms(x_ref[...], g_ref[...])
    x_hi, x_lo = _split_bf16(xn)
    w_hi, w_lo = _split_bf16(w_ref[...])
    logits = (jnp.dot(x_hi, w_hi, preferred_element_type=F32) + jnp.dot(x_hi, w_lo, preferred_element_type=F32)
              + jnp.dot(x_lo, w_hi, preferred_element_type=F32)) + b_ref[...]
    lane = lax.broadcasted_iota(jnp.int32, logits.shape, 1).astype(F32)

    def first_argmax(vals, vmax):
        return jnp.min(jnp.where(vals == vmax, lane, float(ROUTE_LANES)), axis=-1, keepdims=True)

    gl = jnp.where(lane < n_groups, logits, -jnp.inf)
    g_max = jnp.max(gl, axis=-1, keepdims=True)
    g_idx = first_argmax(gl, g_max)
    g_val = 1.0 / jnp.sum(jnp.exp(gl - g_max), axis=-1, keepdims=True)
    lo_lane = n_groups + per_group * g_idx
    el = jnp.where((lane >= lo_lane) & (lane < lo_lane + per_group), logits, -jnp.inf)
    v1 = jnp.max(el, axis=-1, keepdims=True)
    i1 = first_argmax(el, v1)
    el2 = jnp.where(lane == i1, -jnp.inf, el)
    v2 = jnp.max(el2, axis=-1, keepdims=True)
    i2 = first_argmax(el2, v2)
    e21 = jnp.exp(v2 - v1)
    w1 = g_val / (1.0 + e21)
    w2 = g_val * e21 / (1.0 + e21)
    out = jnp.where(lane == 0, i1 - n_groups,
                    jnp.where(lane == 1, i2 - n_groups, jnp.where(lane == 2, w1, jnp.where(lane == 3, w2, 0.0))))
    o_ref[...] = out


def _router(h, g, w_cat, b_cat, tm=256):
    t, d = h.shape
    tm = min(tm, t)
    return pl.pallas_call(
        functools.partial(_router_kernel, n_groups=N_GROUPS, per_group=EXPERTS_PER_GROUP),
        grid=(t // tm,),
        in_specs=[pl.BlockSpec((tm, d), lambda i: (i, 0)), pl.BlockSpec((1, d), lambda i: (0, 0)),
                  pl.BlockSpec((d, ROUTE_LANES), lambda i: (0, 0)), pl.BlockSpec((1, ROUTE_LANES), lambda i: (0, 0))],
        out_specs=pl.BlockSpec((tm, ROUTE_LANES), lambda i: (i, 0)),
        out_shape=jax.ShapeDtypeStruct((t, ROUTE_LANES), F32),
        compiler_params=_cparams(("arbitrary",)),
        name="moe_router",
    )(h, g.reshape(1, d), w_cat, b_cat)


def _row_copy(src_ref, src_row, dst_ref, dst_row, sem):
    return pltpu.make_async_copy(src_ref.at[pl.ds(src_row, 1)], dst_ref.at[pl.ds(dst_row, 1)], sem)


def _dispatch_kernel(pos_ref, x_ref, g_ref, zeros_ref, xs_ref, buf_ref, sem, *, tm):
    del zeros_ref
    base = pl.program_id(0) * (2 * tm)
    buf_ref[...] = _rms(x_ref[...], g_ref[...])

    def start(r, c):
        _row_copy(buf_ref, r, xs_ref, pos_ref[base + 2 * r], sem).start()
        _row_copy(buf_ref, r, xs_ref, pos_ref[base + 2 * r + 1], sem).start()
        return c

    lax.fori_loop(0, tm, start, 0)

    def wait(r, c):
        _row_copy(buf_ref, 0, xs_ref, 0, sem).wait()
        _row_copy(buf_ref, 0, xs_ref, 0, sem).wait()
        return c

    lax.fori_loop(0, tm, wait, 0)


def _dispatch(h, g, pos, rows, tm=256):
    t, d = h.shape
    tm = min(tm, t)
    return pl.pallas_call(
        functools.partial(_dispatch_kernel, tm=tm),
        grid_spec=pltpu.PrefetchScalarGridSpec(
            num_scalar_prefetch=1,
            grid=(t // tm,),
            in_specs=[pl.BlockSpec((tm, d), lambda i, pos: (i, 0)), pl.BlockSpec((1, d), lambda i, pos: (0, 0)),
                      pl.BlockSpec(memory_space=pl.ANY)],
            out_specs=pl.BlockSpec(memory_space=pl.ANY),
            scratch_shapes=[pltpu.VMEM((tm, d), F32), pltpu.SemaphoreType.DMA(())],
        ),
        out_shape=jax.ShapeDtypeStruct((rows, d), F32),
        input_output_aliases={3: 0},
        compiler_params=_cparams(("arbitrary",)),
        name="moe_dispatch",
    )(pos, h, g.reshape(1, d), jnp.zeros((rows, d), F32))


def _expert_up_kernel(ie_ref, if_ref, im_ref, ifirst_ref, ivalid_ref, xs_ref, wg_ref, wu_ref, zeros_ref, o_ref,
                      wgb_ref, wub_ref):
    del ie_ref, if_ref, im_ref, zeros_ref
    i = pl.program_id(0)

    @pl.when(ifirst_ref[i] == 1)
    def _():
        wgb_ref[...] = wg_ref[...].astype(BF16)
        wub_ref[...] = wu_ref[...].astype(BF16)

    @pl.when(ivalid_ref[i] == 1)
    def _():
        x = xs_ref[...].astype(BF16)
        gate = jnp.dot(x, wgb_ref[...], preferred_element_type=F32)
        up = jnp.dot(x, wub_ref[...], preferred_element_type=F32)
        o_ref[...] = (jax.nn.silu(gate) * up).astype(o_ref.dtype)


def _expert_up(xs, w_gate, w_up, layer, items, tm, tf):
    rows, d = xs.shape
    ff = w_gate.shape[-1]
    n_items = items[0].shape[0]
    wspec = pl.BlockSpec((None, None, d, tf), lambda i, ie, jf, im, ifst, iv: (layer, ie[i], 0, jf[i]))
    return pl.pallas_call(
        _expert_up_kernel,
        grid_spec=pltpu.PrefetchScalarGridSpec(
            num_scalar_prefetch=5,
            grid=(n_items,),
            in_specs=[pl.BlockSpec((tm, d), lambda i, ie, jf, im, ifst, iv: (im[i], 0)), wspec, wspec,
                      pl.BlockSpec(memory_space=pl.ANY)],
            out_specs=pl.BlockSpec((tm, tf), lambda i, ie, jf, im, ifst, iv: (im[i], jf[i])),
            scratch_shapes=[pltpu.VMEM((d, tf), BF16), pltpu.VMEM((d, tf), BF16)],
        ),
        out_shape=jax.ShapeDtypeStruct((rows, ff), BF16),
        input_output_aliases={8: 0},
        compiler_params=_cparams(("arbitrary",)),
        name="moe_expert_up",
    )(*items, xs, w_gate, w_up, jnp.zeros((rows, ff), BF16))


def _expert_down_kernel(te_ref, tfirst_ref, tvalid_ref, tsrc_ref, h_ref, wd_ref, o_ref, wdb_ref):
    del te_ref, tsrc_ref
    i = pl.program_id(0)

    @pl.when(tfirst_ref[i] == 1)
    def _():
        wdb_ref[...] = wd_ref[...].astype(BF16)

    @pl.when(tvalid_ref[i] == 1)
    def _():
        o_ref[...] = jnp.dot(h_ref[...], wdb_ref[...], preferred_element_type=F32)

    @pl.when(tvalid_ref[i] == 0)
    def _():
        o_ref[...] = jnp.zeros_like(o_ref)


def _expert_down(hmid, w_down, layer, tiles, tm):
    rows, ff = hmid.shape
    d = w_down.shape[-1]
    n_tiles = rows // tm
    return pl.pallas_call(
        _expert_down_kernel,
        grid_spec=pltpu.PrefetchScalarGridSpec(
            num_scalar_prefetch=4,
            grid=(n_tiles,),
            in_specs=[pl.BlockSpec((tm, ff), lambda i, te, tf, tv, ts: (ts[i], 0)),
                      pl.BlockSpec((None, None, ff, d), lambda i, te, tf, tv, ts: (layer, te[i], 0, 0))],
            out_specs=pl.BlockSpec((tm, d), lambda i, te, tf, tv, ts: (i, 0)),
            scratch_shapes=[pltpu.VMEM((ff, d), BF16)],
        ),
        out_shape=jax.ShapeDtypeStruct((rows, d), F32),
        compiler_params=_cparams(("arbitrary",)),
        name="moe_expert_down",
    )(*tiles, hmid, w_down)


def _combine_kernel(pos_ref, h_ref, route_ref, ys_ref, o_ref, buf_ref, sem, *, tm):
    base = pl.program_id(0) * (2 * tm)

    def start(r, c):
        _row_copy(ys_ref, pos_ref[base + 2 * r], buf_ref.at[0], r, sem).start()
        _row_copy(ys_ref, pos_ref[base + 2 * r + 1], buf_ref.at[1], r, sem).start()
        return c

    lax.fori_loop(0, tm, start, 0)

    def wait(r, c):
        _row_copy(ys_ref, 0, buf_ref.at[0], 0, sem).wait()
        _row_copy(ys_ref, 0, buf_ref.at[1], 0, sem).wait()
        return c

    lax.fori_loop(0, tm, wait, 0)
    route = route_ref[...]
    o_ref[...] = h_ref[...] + route[:, 2:3] * buf_ref[0] + route[:, 3:4] * buf_ref[1]


def _combine(h, route, ys, pos, tm=256):
    t, d = h.shape
    tm = min(tm, t)
    return pl.pallas_call(
        functools.partial(_combine_kernel, tm=tm),
        grid_spec=pltpu.PrefetchScalarGridSpec(
            num_scalar_prefetch=1,
            grid=(t // tm,),
            in_specs=[pl.BlockSpec((tm, d), lambda i, pos: (i, 0)),
                      pl.BlockSpec((tm, ROUTE_LANES), lambda i, pos: (i, 0)),
                      pl.BlockSpec(memory_space=pl.ANY)],
            out_specs=pl.BlockSpec((tm, d), lambda i, pos: (i, 0)),
            scratch_shapes=[pltpu.VMEM((2, tm, d), F32), pltpu.SemaphoreType.DMA(())],
        ),
        out_shape=jax.ShapeDtypeStruct((t, d), F32),
        compiler_params=_cparams(("arbitrary",)),
        name="moe_combine",
    )(pos, h, route, ys)


def _moe_plan(route, n_experts, tm, n_f):
    t = route.shape[0]
    eid = route[:, :2].astype(jnp.int32).reshape(-1)
    onehot = (eid[:, None] == jnp.arange(n_experts, dtype=jnp.int32)[None, :]).astype(jnp.int32)
    csum = jnp.cumsum(onehot, axis=0)
    rank = jnp.sum((csum - 1) * onehot, axis=1)
    counts = csum[-1]
    n_tiles_e = (counts + tm - 1) // tm
    tile_start = jnp.cumsum(n_tiles_e) - n_tiles_e
    pos = (tile_start * tm)[eid] + rank
    n_tiles = (2 * t) // tm + n_experts
    used = jnp.sum(n_tiles_e)
    tile_ids = jnp.arange(n_tiles, dtype=jnp.int32)
    tile_e_raw = jnp.searchsorted(jnp.cumsum(n_tiles_e), tile_ids, side="right").astype(jnp.int32)
    tile_valid = (tile_ids < used).astype(jnp.int32)
    last_e = jnp.max(jnp.where(n_tiles_e > 0, jnp.arange(n_experts, dtype=jnp.int32), 0))
    tile_e = jnp.where(tile_valid == 1, tile_e_raw, last_e)
    tile_first = ((tile_ids == tile_start[tile_e]) & (tile_valid == 1)).astype(jnp.int32)
    tile_src = jnp.minimum(tile_ids, used - 1).astype(jnp.int32)
    n_items = n_tiles * n_f
    item_ids = jnp.arange(n_items, dtype=jnp.int32)
    item_end = jnp.cumsum(n_tiles_e * n_f)
    item_e_raw = jnp.searchsorted(item_end, item_ids, side="right").astype(jnp.int32)
    item_valid = (item_ids < used * n_f).astype(jnp.int32)
    item_e = jnp.where(item_valid == 1, item_e_raw, last_e)
    local = item_ids - (item_end - n_tiles_e * n_f)[item_e]
    nte = jnp.maximum(n_tiles_e[item_e], 1)
    item_f = jnp.where(item_valid == 1, local // nte, n_f - 1).astype(jnp.int32)
    item_m = jnp.where(item_valid == 1, tile_start[item_e] + local % nte, used - 1).astype(jnp.int32)
    item_first = ((local % nte == 0) & (item_valid == 1)).astype(jnp.int32)
    return (pos.astype(jnp.int32), (tile_e, tile_first, tile_valid, tile_src),
            (item_e, item_f, item_m, item_first, item_valid))


def _hier_moe(h, g, w_cat, b_cat, w_gate, w_up, w_down, layer, tm=256, tf=512):
    t, d = h.shape
    n_experts = w_gate.shape[1]
    ff = w_gate.shape[-1]
    tm = min(tm, t)
    tf = min(tf, ff)
    rows = 2 * t + n_experts * tm
    route = _router(h, g, w_cat, b_cat)
    pos, tiles, items = _moe_plan(route, n_experts, tm, ff // tf)
    xs = _dispatch(h, g, pos, rows, tm)
    hmid = _expert_up(xs, w_gate, w_up, layer, items, tm, tf)
    ys = _expert_down(hmid, w_down, layer, tiles, tm)
    return _combine(h, route, ys, pos, tm)


def kernel(x, ssm_w_in, ssm_a_re, ssm_a_im, ssm_log_dt, ssm_b_re, ssm_b_im, ssm_c_re, ssm_c_im, ssm_d, ssm_w_glu_a, ssm_w_glu_b, kv_norm_g, w_k, w_v, attn_w_q, attn_lambda_q1, attn_lambda_k1, attn_lambda_q2, attn_lambda_k2, attn_subln_g, attn_w_o, moe_w_group, moe_b_group, moe_w_router, moe_b_router, moe_w_gate, moe_w_up, moe_w_down, norm_mix_g, norm_ffn_g, final_norm_g):
    bsz, seq, d = x.shape
    t = bsz * seq
    depth = norm_mix_g.shape[0]
    n_a = ssm_w_in.shape[0]
    head_dim = attn_lambda_q1.shape[1]
    n_heads = attn_w_q.shape[-1] // (2 * head_dim)
    n_experts = moe_w_router.shape[-1]
    h = x.reshape(t, d)

    def moe(h, li):
        w_cat = jnp.zeros((d, ROUTE_LANES), F32)
        w_cat = w_cat.at[:, :N_GROUPS].set(moe_w_group[li]).at[:, N_GROUPS:N_GROUPS + n_experts].set(moe_w_router[li])
        b_cat = jnp.zeros((1, ROUTE_LANES), F32)
        b_cat = b_cat.at[0, :N_GROUPS].set(moe_b_group[li]).at[0, N_GROUPS:N_GROUPS + n_experts].set(moe_b_router[li])
        return _hier_moe(h, norm_ffn_g[li], w_cat, b_cat, moe_w_gate, moe_w_up, moe_w_down, li)

    k_shared = v_shared = None
    for li in range(depth):
        hn = _rmsnorm(h, norm_mix_g[li], BF16)
        if li < n_a:
            a = li
            tabs = _ssm_tables(ssm_a_re[a], ssm_a_im[a], ssm_log_dt[a], ssm_b_re[a], ssm_b_im[a],
                               ssm_c_re[a], ssm_c_im[a], ssm_d[a], SSM_CHUNK)
            u = _matmul(hn, ssm_w_in[a], BF16)
            y = _s5_core(u, bsz, seq, tabs, SSM_CHUNK)
            h = _glu_res(y, ssm_w_glu_a[a], ssm_w_glu_b[a], h)
        else:
            b = li - n_a
            lambda_init = 0.8 - 0.6 * math.exp(-0.3 * li)
            lam = (jnp.exp(jnp.sum(attn_lambda_q1[b] * attn_lambda_k1[b]))
                   - jnp.exp(jnp.sum(attn_lambda_q2[b] * attn_lambda_k2[b])) + lambda_init).reshape(1)
            slopes = 2.0 ** (-8.0 * jnp.arange(1, n_heads + 1, dtype=F32) / n_heads)
            q = _matmul(hn, attn_w_q[b], BF16, scale=head_dim ** -0.5)
            o = _diff_attention(q, k_shared, v_shared, lam, slopes, attn_subln_g[b], bsz, seq, n_heads,
                                1.0 - lambda_init)
            h = _matmul_res(o, attn_w_o[b], h)
        h = moe(h, li)
        if li == n_a - 1:
            kvn = _rmsnorm(h, kv_norm_g, BF16)
            k_shared = _matmul(kvn, w_k, BF16)
            v_shared = _matmul(kvn, w_v, BF16)
    return _rmsnorm(h, final_norm_g, F32).reshape(bsz, seq, d)
```

```python
import functools
import math

import jax
import jax.numpy as jnp
from jax import lax
from jax.experimental import pallas as pl
from jax.experimental.pallas import tpu as pltpu

EPS = 1e-6
F32 = jnp.float32
BF16 = jnp.bfloat16

V7X_VMEM_BYTES = 64 * 1024 * 1024
VMEM_LIMIT = V7X_VMEM_BYTES - 8 * 1024 * 1024
LANES = 128

SSM_CHUNK = 16
N_GROUPS = 4
EXPERTS_PER_GROUP = 4
ROUTE_LANES = LANES


def _cparams(sem, vmem=VMEM_LIMIT):
    return pltpu.CompilerParams(dimension_semantics=sem, vmem_limit_bytes=vmem)


def _rms(x, g):
    inv = lax.rsqrt(jnp.mean(x * x, axis=-1, keepdims=True) + EPS)
    return x * inv * g


def _rmsnorm_kernel(x_ref, g_ref, o_ref):
    o_ref[...] = _rms(x_ref[...], g_ref[...]).astype(o_ref.dtype)


def _rmsnorm(x, g, out_dtype, tm=256):
    t, d = x.shape
    tm = min(tm, t)
    return pl.pallas_call(
        _rmsnorm_kernel,
        grid=(t // tm,),
        in_specs=[pl.BlockSpec((tm, d), lambda i: (i, 0)), pl.BlockSpec((1, d), lambda i: (0, 0))],
        out_specs=pl.BlockSpec((tm, d), lambda i: (i, 0)),
        out_shape=jax.ShapeDtypeStruct((t, d), out_dtype),
        compiler_params=_cparams(("arbitrary",)),
        name="rmsnorm",
    )(x, g.reshape(1, d))


def _mm_kernel(x_ref, w_ref, o_ref, wb_ref, *, scale):
    @pl.when(pl.program_id(1) == 0)
    def _():
        wb_ref[...] = w_ref[...].astype(BF16)

    acc = jnp.dot(x_ref[...], wb_ref[...], preferred_element_type=F32)
    if scale != 1.0:
        acc = acc * scale
    o_ref[...] = acc.astype(o_ref.dtype)


def _mm_res_kernel(x_ref, w_ref, r_ref, o_ref, wb_ref):
    @pl.when(pl.program_id(1) == 0)
    def _():
        wb_ref[...] = w_ref[...].astype(BF16)

    o_ref[...] = r_ref[...] + jnp.dot(x_ref[...], wb_ref[...], preferred_element_type=F32)


def _glu_res_kernel(x_ref, wa_ref, wb_ref, r_ref, o_ref, wab_ref, wbb_ref):
    @pl.when(pl.program_id(1) == 0)
    def _():
        wab_ref[...] = wa_ref[...].astype(BF16)
        wbb_ref[...] = wb_ref[...].astype(BF16)

    x = x_ref[...].astype(BF16)
    a = jnp.dot(x, wab_ref[...], preferred_element_type=F32)
    b = jnp.dot(x, wbb_ref[...], preferred_element_type=F32)
    o_ref[...] = r_ref[...] + a * jax.nn.sigmoid(b)


def _mm_tiles(m, n, tm, tn):
    return min(tm, m), min(tn, n)


def _matmul(x, w, out_dtype, scale=1.0, tm=1024, tn=512):
    m, k = x.shape
    n = w.shape[1]
    tm, tn = _mm_tiles(m, n, tm, tn)
    return pl.pallas_call(
        functools.partial(_mm_kernel, scale=scale),
        grid=(n // tn, m // tm),
        in_specs=[pl.BlockSpec((tm, k), lambda j, i: (i, 0)), pl.BlockSpec((k, tn), lambda j, i: (0, j))],
        out_specs=pl.BlockSpec((tm, tn), lambda j, i: (i, j)),
        out_shape=jax.ShapeDtypeStruct((m, n), out_dtype),
        scratch_shapes=[pltpu.VMEM((k, tn), BF16)],
        compiler_params=_cparams(("arbitrary", "arbitrary")),
        name="matmul",
    )(x, w)


def _matmul_res(x, w, res, tm=1024, tn=512):
    m, k = x.shape
    n = w.shape[1]
    tm, tn = _mm_tiles(m, n, tm, tn)
    return pl.pallas_call(
        _mm_res_kernel,
        grid=(n // tn, m // tm),
        in_specs=[pl.BlockSpec((tm, k), lambda j, i: (i, 0)), pl.BlockSpec((k, tn), lambda j, i: (0, j)),
                  pl.BlockSpec((tm, tn), lambda j, i: (i, j))],
        out_specs=pl.BlockSpec((tm, tn), lambda j, i: (i, j)),
        out_shape=jax.ShapeDtypeStruct((m, n), F32),
        scratch_shapes=[pltpu.VMEM((k, tn), BF16)],
        compiler_params=_cparams(("arbitrary", "arbitrary")),
        name="matmul_res",
    )(x, w, res)


def _glu_res(x, wa, wb, res, tm=1024, tn=512):
    m, k = x.shape
    n = wa.shape[1]
    tm, tn = _mm_tiles(m, n, tm, tn)
    wspec = pl.BlockSpec((k, tn), lambda j, i: (0, j))
    return pl.pallas_call(
        _glu_res_kernel,
        grid=(n // tn, m // tm),
        in_specs=[pl.BlockSpec((tm, k), lambda j, i: (i, 0)), wspec, wspec,
                  pl.BlockSpec((tm, tn), lambda j, i: (i, j))],
        out_specs=pl.BlockSpec((tm, tn), lambda j, i: (i, j)),
        out_shape=jax.ShapeDtypeStruct((m, n), F32),
        scratch_shapes=[pltpu.VMEM((k, tn), BF16), pltpu.VMEM((k, tn), BF16)],
        compiler_params=_cparams(("arbitrary", "arbitrary")),
        name="glu_res",
    )(x, wa, wb, res)


def _ssm_tables(a_re, a_im, log_dt, b_re, b_im, c_re, c_im, d_skip, c):
    hp = lax.Precision.HIGHEST
    g_n, p_n = a_re.shape
    j_n = d_skip.shape[1]
    dt = jnp.exp(log_dt)[:, None]
    steps = jnp.arange(c + 1, dtype=F32)[:, None, None]
    mag = jnp.exp(steps * (a_re * dt)[None])
    ang = steps * (a_im * dt)[None]
    pw_re = mag * jnp.cos(ang)
    pw_im = mag * jnp.sin(ang)
    lb_re, lb_im = pw_re[1], pw_im[1]
    den = a_re * a_re + a_im * a_im
    nr = lb_re - 1.0
    q_re = (nr * a_re + lb_im * a_im) / den
    q_im = (lb_im * a_re - nr * a_im) / den
    bb_re = q_re[..., None] * b_re - q_im[..., None] * b_im
    bb_im = q_re[..., None] * b_im + q_im[..., None] * b_re
    cp_re = c_re[None] * pw_re[:, :, None, :] - c_im[None] * pw_im[:, :, None, :]
    cp_im = c_re[None] * pw_im[:, :, None, :] + c_im[None] * pw_re[:, :, None, :]
    kern = (jnp.einsum('dgjp,gpk->gdkj', cp_re[:c], bb_re, precision=hp)
            - jnp.einsum('dgjp,gpk->gdkj', cp_im[:c], bb_im, precision=hp))
    kern = kern.at[:, 0].add(d_skip[:, :, None] * jnp.eye(j_n, dtype=F32))
    gpb = LANES // j_n
    nb = g_n // gpb
    eye = jnp.eye(gpb, dtype=F32)
    lag = kern.reshape(nb, gpb, c, j_n, j_n).transpose(0, 2, 1, 3, 4)[:, :, :, :, None, :] * eye[None, None, :, None, :, None]
    lag = lag.reshape(nb, c, gpb * j_n, gpb * j_n)
    rp_re = pw_re[:c][::-1].transpose(1, 0, 2)[:, :, None, :]
    rp_im = pw_im[:c][::-1].transpose(1, 0, 2)[:, :, None, :]
    bbt_re = bb_re.transpose(0, 2, 1)[:, None]
    bbt_im = bb_im.transpose(0, 2, 1)[:, None]
    w_in = jnp.stack([rp_re * bbt_re - rp_im * bbt_im, rp_re * bbt_im + rp_im * bbt_re], axis=-2)
    w_in = w_in.reshape(nb, gpb, c, j_n, 2, p_n).transpose(0, 2, 1, 3, 4, 5)
    w_in = w_in[:, :, :, :, :, None, :] * eye[None, None, :, None, None, :, None]
    w_in = w_in.reshape(nb, c * gpb * j_n, 2 * gpb * p_n)
    w_out = jnp.stack([cp_re[1:], -cp_im[1:]], axis=0)
    w_out = w_out.reshape(2, c, nb, gpb, j_n, p_n).transpose(2, 0, 3, 5, 1, 4)
    w_out = w_out[:, :, :, :, :, None, :] * eye[None, None, :, None, None, :, None]
    w_out = w_out.reshape(nb, 2 * gpb * p_n, c * gpb * j_n)
    r1 = jnp.arange(1, 9, dtype=F32)[:, None, None] * float(c)
    mag8 = jnp.exp(r1 * (a_re * dt)[None])
    ang8 = r1 * (a_im * dt)[None]
    lam_pows = jnp.stack([mag8 * jnp.cos(ang8), mag8 * jnp.sin(ang8)], axis=1)
    lam_pows = lam_pows.reshape(8, 2, nb, gpb * p_n).transpose(2, 0, 1, 3).reshape(nb, 8, 2 * gpb * p_n)
    return dict(lag=lag.astype(BF16), w_in=w_in.astype(BF16), w_out=w_out.astype(BF16), lam_pows=lam_pows)


def _chunk_rows(u_ref, c):
    m = u_ref.shape[0] // c
    return jnp.concatenate([u_ref[pl.ds(i, m, stride=c), :].astype(BF16) for i in range(c)], axis=-1)


def _ssm_in_kernel(u_ref, win_ref, x_ref, *, c):
    x_ref[...] = jnp.dot(_chunk_rows(u_ref, c), win_ref[0], preferred_element_type=F32)


def _ssm_scan_kernel(x_ref, lp_ref, s_ref, loc_ref, *, bsz, n_chunks):
    hw = x_ref.shape[1] // 2
    ng = n_chunks // 8
    lp = lp_ref[0]
    sub = lax.broadcasted_iota(jnp.int32, (1, 8, hw), 1)

    def shifted(v, s):
        return jnp.where(sub >= s, pltpu.roll(v, s, axis=1), 0.0)

    xr = x_ref[:, :hw].reshape(bsz * ng, 8, hw)
    xi = x_ref[:, hw:].reshape(bsz * ng, 8, hw)
    for s in (1, 2, 4):
        lr = lp[s - 1:s, :hw].reshape(1, 1, hw)
        li = lp[s - 1:s, hw:].reshape(1, 1, hw)
        yr, yi = shifted(xr, s), shifted(xi, s)
        xr, xi = xr + lr * yr - li * yi, xi + lr * yi + li * yr
    loc_ref[:, :hw] = xr.reshape(bsz * n_chunks, hw)
    loc_ref[:, hw:] = xi.reshape(bsz * n_chunks, hw)
    s_ref[:, :hw] = shifted(xr, 1).reshape(bsz * n_chunks, hw)
    s_ref[:, hw:] = shifted(xi, 1).reshape(bsz * n_chunks, hw)

    sub2 = lax.broadcasted_iota(jnp.int32, (8, hw), 0)
    p0r = jnp.where(sub2 >= 1, pltpu.roll(lp[:, :hw], 1, axis=0), 1.0)
    p0i = jnp.where(sub2 >= 1, pltpu.roll(lp[:, hw:], 1, axis=0), 0.0)
    l8r = lp[7:8, :hw]
    l8i = lp[7:8, hw:]

    def group(k, carry):
        new = []
        for b in range(bsz):
            cr, ci = carry[2 * b], carry[2 * b + 1]
            row0 = pl.multiple_of((b * ng + k) * 8, 8)
            s_ref[pl.ds(row0, 8), :hw] = s_ref[pl.ds(row0, 8), :hw] + p0r * cr - p0i * ci
            s_ref[pl.ds(row0, 8), hw:] = s_ref[pl.ds(row0, 8), hw:] + p0r * ci + p0i * cr
            er = loc_ref[pl.ds(row0 + 7, 1), :hw]
            ei = loc_ref[pl.ds(row0 + 7, 1), hw:]
            new += [er + l8r * cr - l8i * ci, ei + l8r * ci + l8i * cr]
        return tuple(new)

    zero = jnp.zeros((1, hw), F32)
    lax.fori_loop(0, ng, group, (zero,) * (2 * bsz))


def _ssm_out_kernel(u_ref, lag_ref, s_ref, wout_ref, y_ref, w_ref, *, c):
    m = u_ref.shape[0] // c
    lw = u_ref.shape[1]

    @pl.when(pl.program_id(0) == 0)
    def _():
        w_ref[...] = jnp.zeros_like(w_ref)

    for ip in range(c):
        for i in range(ip, c):
            w_ref[ip * lw:(ip + 1) * lw, i * lw:(i + 1) * lw] = lag_ref[0, i - ip]
    y = (jnp.dot(_chunk_rows(u_ref, c), w_ref[...], preferred_element_type=F32)
         + jnp.dot(s_ref[...].astype(BF16), wout_ref[0], preferred_element_type=F32))
    y = jax.nn.gelu(y)
    for i in range(c):
        y_ref[pl.ds(i, m, stride=c), :] = y[:, i * lw:(i + 1) * lw]


def _s5_core(u, bsz, seq, tabs, c):
    t, width = u.shape
    nb, _, lw, _ = tabs["lag"].shape
    sw = tabs["w_in"].shape[2]
    n_chunks = seq // c
    m = bsz * n_chunks
    u_spec = pl.BlockSpec((t, lw), lambda b: (0, b))
    x_spec = pl.BlockSpec((m, sw), lambda b: (0, b))
    x_shape = jax.ShapeDtypeStruct((m, nb * sw), F32)
    x = pl.pallas_call(
        functools.partial(_ssm_in_kernel, c=c),
        grid=(nb,),
        in_specs=[u_spec, pl.BlockSpec((1, c * lw, sw), lambda b: (b, 0, 0))],
        out_specs=x_spec,
        out_shape=x_shape,
        compiler_params=_cparams(("arbitrary",)),
        name="ssm_chunk_in",
    )(u, tabs["w_in"])
    s = pl.pallas_call(
        functools.partial(_ssm_scan_kernel, bsz=bsz, n_chunks=n_chunks),
        grid=(nb,),
        in_specs=[x_spec, pl.BlockSpec((1, 8, sw), lambda b: (b, 0, 0))],
        out_specs=x_spec,
        out_shape=x_shape,
        scratch_shapes=[pltpu.VMEM((m, sw), F32)],
        compiler_params=_cparams(("arbitrary",)),
        name="ssm_chunk_scan",
    )(x, tabs["lam_pows"])
    return pl.pallas_call(
        functools.partial(_ssm_out_kernel, c=c),
        grid=(nb,),
        in_specs=[u_spec, pl.BlockSpec((1, c, lw, lw), lambda b: (b, 0, 0, 0)), x_spec,
                  pl.BlockSpec((1, sw, c * lw), lambda b: (b, 0, 0))],
        out_specs=u_spec,
        out_shape=jax.ShapeDtypeStruct((t, width), F32),
        scratch_shapes=[pltpu.VMEM((c * lw, c * lw), BF16)],
        compiler_params=_cparams(("arbitrary",)),
        name="ssm_chunk_out",
    )(u, tabs["lag"], s, tabs["w_out"])


def _attn_kernel(lam_ref, slope_ref, q_ref, k_ref, v_ref, g_ref, o_ref, acc0_ref, acc1_ref, *, tq, post_scale):
    h = pl.program_id(1)
    qi = pl.program_id(2)
    slope = slope_ref[h]
    lam = lam_ref[0]
    dh = q_ref.shape[1] // 2
    q0 = q_ref[:, :dh]
    q1 = q_ref[:, dh:]
    col = lax.broadcasted_iota(jnp.int32, (1, tq), 1).astype(F32)
    nt = (((1,), (1,)), ((), ()))

    def scores(j):
        ks = pl.multiple_of(j * tq, tq)
        kt = k_ref[pl.ds(ks, tq), :]
        vt = v_ref[pl.ds(ks, tq), :]
        kb = slope * (col + ((j - qi) * tq).astype(F32))
        s0 = lax.dot_general(q0, kt[:, :dh], nt, preferred_element_type=F32) + kb
        s1 = lax.dot_general(q1, kt[:, dh:], nt, preferred_element_type=F32) + kb
        return s0, s1, vt

    def update(s, m, l, acc_ref, vt):
        m_new = jnp.maximum(m, jnp.max(s, axis=-1, keepdims=True))
        alpha = jnp.exp(m - m_new)
        p = jnp.exp(s - m_new)
        l_new = alpha * l + jnp.sum(p, axis=-1, keepdims=True)
        acc_ref[...] = alpha * acc_ref[...] + jnp.dot(p.astype(BF16), vt, preferred_element_type=F32)
        return m_new, l_new

    acc0_ref[...] = jnp.zeros_like(acc0_ref)
    acc1_ref[...] = jnp.zeros_like(acc1_ref)

    def body(j, carry):
        m0, l0, m1, l1 = carry
        s0, s1, vt = scores(j)
        m0, l0 = update(s0, m0, l0, acc0_ref, vt)
        m1, l1 = update(s1, m1, l1, acc1_ref, vt)
        return m0, l0, m1, l1

    neg = jnp.full((tq, 1), -1e30, F32)
    zero = jnp.zeros((tq, 1), F32)
    m0, l0, m1, l1 = lax.fori_loop(0, qi, body, (neg, zero, neg, zero))

    s0, s1, vt = scores(qi)
    causal = lax.broadcasted_iota(jnp.int32, (tq, tq), 0) >= lax.broadcasted_iota(jnp.int32, (tq, tq), 1)
    m0, l0 = update(jnp.where(causal, s0, -jnp.inf), m0, l0, acc0_ref, vt)
    m1, l1 = update(jnp.where(causal, s1, -jnp.inf), m1, l1, acc1_ref, vt)

    o = acc0_ref[...] / l0 - lam * (acc1_ref[...] / l1)
    o_ref[...] = (_rms(o, g_ref[...]) * post_scale).astype(o_ref.dtype)


def _diff_attention(q, k, v, lam, slopes, subln_g, bsz, seq, n_heads, post_scale, tq=512):
    t, width = q.shape
    hw = width // n_heads
    tq = min(tq, seq)
    nq = seq // tq
    return pl.pallas_call(
        functools.partial(_attn_kernel, tq=tq, post_scale=post_scale),
        grid=(bsz, n_heads, nq),
        in_specs=[pl.BlockSpec(memory_space=pltpu.SMEM), pl.BlockSpec(memory_space=pltpu.SMEM),
                  pl.BlockSpec((tq, hw), lambda b, h, i: (b * nq + i, h)),
                  pl.BlockSpec((seq, hw), lambda b, h, i: (b, h)),
                  pl.BlockSpec((seq, hw), lambda b, h, i: (b, h)),
                  pl.BlockSpec((1, hw), lambda b, h, i: (0, 0))],
        out_specs=pl.BlockSpec((tq, hw), lambda b, h, i: (b * nq + i, h)),
        out_shape=jax.ShapeDtypeStruct((t, width), BF16),
        scratch_shapes=[pltpu.VMEM((tq, hw), F32), pltpu.VMEM((tq, hw), F32)],
        compiler_params=_cparams(("arbitrary", "arbitrary", "arbitrary")),
        name="diff_attention",
    )(lam, slopes, q, k, v, subln_g.reshape(1, hw))


def _split_bf16(x):
    hi = x.astype(BF16)
    lo = (x - hi.astype(F32)).astype(BF16)
    return hi, lo


def _router_kernel(x_ref, g_ref, w_ref, b_ref, o_ref, cnt_out_ref, cnt_ref, *, n_groups, per_group):
    xn = _rms(x_ref[...], g_ref[...])
    x_hi, x_lo = _split_bf16(xn)
    w_hi, w_lo = _split_bf16(w_ref[...])
    logits = (jnp.dot(x_hi, w_hi, preferred_element_type=F32) + jnp.dot(x_hi, w_lo, preferred_element_type=F32)
              + jnp.dot(x_lo, w_hi, preferred_element_type=F32)) + b_ref[...]
    lane = lax.broadcasted_iota(jnp.int32, logits.shape, 1).astype(F32)

    def first_argmax(vals, vmax):
        return jnp.min(jnp.where(vals == vmax, lane, float(ROUTE_LANES)), axis=-1, keepdims=True)

    gl = jnp.where(lane < n_groups, logits, -jnp.inf)
    g_max = jnp.max(gl, axis=-1, keepdims=True)
    g_idx = first_argmax(gl, g_max)
    g_val = 1.0 / jnp.sum(jnp.exp(gl - g_max), axis=-1, keepdims=True)
    lo_lane = n_groups + per_group * g_idx
    el = jnp.where((lane >= lo_lane) & (lane < lo_lane + per_group), logits, -jnp.inf)
    v1 = jnp.max(el, axis=-1, keepdims=True)
    i1 = first_argmax(el, v1)
    el2 = jnp.where(lane == i1, -jnp.inf, el)
    v2 = jnp.max(el2, axis=-1, keepdims=True)
    i2 = first_argmax(el2, v2)
    e21 = jnp.exp(v2 - v1)
    w1 = g_val / (1.0 + e21)
    w2 = g_val * e21 / (1.0 + e21)
    e1 = i1 - n_groups
    e2 = i2 - n_groups
    @pl.when(pl.program_id(0) == 0)
    def _():
        cnt_ref[...] = jnp.zeros_like(cnt_ref)

    tm = logits.shape[0]
    hit1 = lane == e1
    hit2 = lane == e2
    onehot = jnp.where(hit1 | hit2, 1.0, 0.0)
    tri = jnp.where(lax.broadcasted_iota(jnp.int32, (tm, tm), 0) > lax.broadcasted_iota(jnp.int32, (tm, tm), 1),
                    1.0, 0.0).astype(BF16)
    before = jnp.dot(tri, onehot.astype(BF16), preferred_element_type=F32) + cnt_ref[0:1, :]
    rank1 = jnp.sum(jnp.where(hit1, before, 0.0), axis=-1, keepdims=True)
    rank2 = jnp.sum(jnp.where(hit2, before, 0.0), axis=-1, keepdims=True)
    cnt_ref[...] = cnt_ref[...] + jnp.sum(onehot, axis=0, keepdims=True)
    cnt_out_ref[...] = cnt_ref[...]
    out = jnp.where(lane == 0, e1, jnp.where(lane == 1, e2, jnp.where(lane == 2, w1, jnp.where(lane == 3, w2,
                    jnp.where(lane == 4, rank1, jnp.where(lane == 5, rank2, 0.0))))))
    o_ref[...] = out


def _router(h, g, w_cat, b_cat, tm=256):
    t, d = h.shape
    tm = min(tm, t)
    return pl.pallas_call(
        functools.partial(_router_kernel, n_groups=N_GROUPS, per_group=EXPERTS_PER_GROUP),
        grid=(t // tm,),
        in_specs=[pl.BlockSpec((tm, d), lambda i: (i, 0)), pl.BlockSpec((1, d), lambda i: (0, 0)),
                  pl.BlockSpec((d, ROUTE_LANES), lambda i: (0, 0)), pl.BlockSpec((1, ROUTE_LANES), lambda i: (0, 0))],
        out_specs=[pl.BlockSpec((tm, ROUTE_LANES), lambda i: (i, 0)), pl.BlockSpec((8, ROUTE_LANES), lambda i: (0, 0))],
        out_shape=[jax.ShapeDtypeStruct((t, ROUTE_LANES), F32), jax.ShapeDtypeStruct((8, ROUTE_LANES), F32)],
        scratch_shapes=[pltpu.VMEM((8, ROUTE_LANES), F32)],
        compiler_params=_cparams(("arbitrary",)),
        name="moe_router",
    )(h, g.reshape(1, d), w_cat, b_cat)


def _row_copy(src_ref, src_row, dst_ref, dst_row, sem):
    return pltpu.make_async_copy(src_ref.at[pl.ds(src_row, 1)], dst_ref.at[pl.ds(dst_row, 1)], sem)


def _dispatch_kernel(pos_ref, x_ref, g_ref, zeros_ref, xs_ref, buf_ref, sem, *, tm):
    del zeros_ref
    base = pl.program_id(0) * (2 * tm)
    buf_ref[...] = _rms(x_ref[...], g_ref[...])

    def start(r, c):
        _row_copy(buf_ref, r, xs_ref, pos_ref[base + 2 * r], sem).start()
        _row_copy(buf_ref, r, xs_ref, pos_ref[base + 2 * r + 1], sem).start()
        return c

    lax.fori_loop(0, tm, start, 0)

    def wait(r, c):
        _row_copy(buf_ref, 0, xs_ref, 0, sem).wait()
        _row_copy(buf_ref, 0, xs_ref, 0, sem).wait()
        return c

    lax.fori_loop(0, tm, wait, 0)


def _dispatch(h, g, pos, rows, tm=256):
    t, d = h.shape
    tm = min(tm, t)
    return pl.pallas_call(
        functools.partial(_dispatch_kernel, tm=tm),
        grid_spec=pltpu.PrefetchScalarGridSpec(
            num_scalar_prefetch=1,
            grid=(t // tm,),
            in_specs=[pl.BlockSpec((tm, d), lambda i, pos: (i, 0)), pl.BlockSpec((1, d), lambda i, pos: (0, 0)),
                      pl.BlockSpec(memory_space=pl.ANY)],
            out_specs=pl.BlockSpec(memory_space=pl.ANY),
            scratch_shapes=[pltpu.VMEM((tm, d), F32), pltpu.SemaphoreType.DMA(())],
        ),
        out_shape=jax.ShapeDtypeStruct((rows, d), F32),
        input_output_aliases={3: 0},
        compiler_params=_cparams(("arbitrary",)),
        name="moe_dispatch",
    )(pos, h, g.reshape(1, d), jnp.zeros((rows, d), F32))


def _expert_up_kernel(ie_ref, if_ref, im_ref, ifirst_ref, ivalid_ref, xs_ref, wg_ref, wu_ref, zeros_ref, o_ref,
                      wgb_ref, wub_ref):
    del ie_ref, if_ref, im_ref, zeros_ref
    i = pl.program_id(0)

    @pl.when(ifirst_ref[i] == 1)
    def _():
        wgb_ref[...] = wg_ref[...].astype(BF16)
        wub_ref[...] = wu_ref[...].astype(BF16)

    @pl.when(ivalid_ref[i] == 1)
    def _():
        x = xs_ref[...].astype(BF16)
        gate = jnp.dot(x, wgb_ref[...], preferred_element_type=F32)
        up = jnp.dot(x, wub_ref[...], preferred_element_type=F32)
        o_ref[...] = (jax.nn.silu(gate) * up).astype(o_ref.dtype)


def _expert_up(xs, w_gate, w_up, layer, items, tm, tf):
    rows, d = xs.shape
    ff = w_gate.shape[-1]
    n_items = items[0].shape[0]
    wspec = pl.BlockSpec((None, None, d, tf), lambda i, ie, jf, im, ifst, iv: (layer, ie[i], 0, jf[i]))
    return pl.pallas_call(
        _expert_up_kernel,
        grid_spec=pltpu.PrefetchScalarGridSpec(
            num_scalar_prefetch=5,
            grid=(n_items,),
            in_specs=[pl.BlockSpec((tm, d), lambda i, ie, jf, im, ifst, iv: (im[i], 0)), wspec, wspec,
                      pl.BlockSpec(memory_space=pl.ANY)],
            out_specs=pl.BlockSpec((tm, tf), lambda i, ie, jf, im, ifst, iv: (im[i], jf[i])),
            scratch_shapes=[pltpu.VMEM((d, tf), BF16), pltpu.VMEM((d, tf), BF16)],
        ),
        out_shape=jax.ShapeDtypeStruct((rows, ff), BF16),
        input_output_aliases={8: 0},
        compiler_params=_cparams(("arbitrary",)),
        name="moe_expert_up",
    )(*items, xs, w_gate, w_up, jnp.zeros((rows, ff), BF16))


def _expert_down_kernel(te_ref, tfirst_ref, tvalid_ref, tsrc_ref, h_ref, wd_ref, o_ref, wdb_ref):
    del te_ref, tsrc_ref
    i = pl.program_id(0)

    @pl.when(tfirst_ref[i] == 1)
    def _():
        wdb_ref[...] = wd_ref[...].astype(BF16)

    @pl.when(tvalid_ref[i] == 1)
    def _():
        o_ref[...] = jnp.dot(h_ref[...], wdb_ref[...], preferred_element_type=F32)

    @pl.when(tvalid_ref[i] == 0)
    def _():
        o_ref[...] = jnp.zeros_like(o_ref)


def _expert_down(hmid, w_down, layer, tiles, tm):
    rows, ff = hmid.shape
    d = w_down.shape[-1]
    n_tiles = rows // tm
    return pl.pallas_call(
        _expert_down_kernel,
        grid_spec=pltpu.PrefetchScalarGridSpec(
            num_scalar_prefetch=4,
            grid=(n_tiles,),
            in_specs=[pl.BlockSpec((tm, ff), lambda i, te, tf, tv, ts: (ts[i], 0)),
                      pl.BlockSpec((None, None, ff, d), lambda i, te, tf, tv, ts: (layer, te[i], 0, 0))],
            out_specs=pl.BlockSpec((tm, d), lambda i, te, tf, tv, ts: (i, 0)),
            scratch_shapes=[pltpu.VMEM((ff, d), BF16)],
        ),
        out_shape=jax.ShapeDtypeStruct((rows, d), F32),
        compiler_params=_cparams(("arbitrary",)),
        name="moe_expert_down",
    )(*tiles, hmid, w_down)


def _combine_kernel(pos_ref, h_ref, route_ref, ys_ref, o_ref, buf_ref, sem, *, tm):
    base = pl.program_id(0) * (2 * tm)

    def start(r, c):
        _row_copy(ys_ref, pos_ref[base + 2 * r], buf_ref.at[0], r, sem).start()
        _row_copy(ys_ref, pos_ref[base + 2 * r + 1], buf_ref.at[1], r, sem).start()
        return c

    lax.fori_loop(0, tm, start, 0)

    def wait(r, c):
        _row_copy(ys_ref, 0, buf_ref.at[0], 0, sem).wait()
        _row_copy(ys_ref, 0, buf_ref.at[1], 0, sem).wait()
        return c

    lax.fori_loop(0, tm, wait, 0)
    route = route_ref[...]
    o_ref[...] = h_ref[...] + route[:, 2:3] * buf_ref[0] + route[:, 3:4] * buf_ref[1]


def _combine(h, route, ys, pos, tm=256):
    t, d = h.shape
    tm = min(tm, t)
    return pl.pallas_call(
        functools.partial(_combine_kernel, tm=tm),
        grid_spec=pltpu.PrefetchScalarGridSpec(
            num_scalar_prefetch=1,
            grid=(t // tm,),
            in_specs=[pl.BlockSpec((tm, d), lambda i, pos: (i, 0)),
                      pl.BlockSpec((tm, ROUTE_LANES), lambda i, pos: (i, 0)),
                      pl.BlockSpec(memory_space=pl.ANY)],
            out_specs=pl.BlockSpec((tm, d), lambda i, pos: (i, 0)),
            scratch_shapes=[pltpu.VMEM((2, tm, d), F32), pltpu.SemaphoreType.DMA(())],
        ),
        out_shape=jax.ShapeDtypeStruct((t, d), F32),
        compiler_params=_cparams(("arbitrary",)),
        name="moe_combine",
    )(pos, h, route, ys)


def _moe_plan(route, counts, n_experts, tm, n_f):
    t = route.shape[0]
    experts = jnp.arange(n_experts, dtype=jnp.int32)

    def take(table, idx):
        return jnp.sum(jnp.where(idx[:, None] == experts[None, :], table[None, :], 0), axis=1)

    eid = route[:, :2].astype(jnp.int32).reshape(-1)
    rank = route[:, 4:6].astype(jnp.int32).reshape(-1)
    counts = counts[0, :n_experts].astype(jnp.int32)
    n_tiles_e = (counts + tm - 1) // tm
    tile_end = jnp.cumsum(n_tiles_e)
    tile_start = tile_end - n_tiles_e
    pos = take(tile_start * tm, eid) + rank
    n_tiles = (2 * t) // tm + n_experts
    used = tile_end[-1]
    tile_ids = jnp.arange(n_tiles, dtype=jnp.int32)
    tile_valid = (tile_ids < used).astype(jnp.int32)
    last_e = jnp.max(jnp.where(n_tiles_e > 0, experts, 0))
    tile_e = jnp.sum((tile_ids[:, None] >= tile_end[None, :]).astype(jnp.int32), axis=1)
    tile_e = jnp.where(tile_valid == 1, tile_e, last_e)
    tile_first = ((tile_ids == take(tile_start, tile_e)) & (tile_valid == 1)).astype(jnp.int32)
    tile_src = jnp.minimum(tile_ids, used - 1).astype(jnp.int32)
    n_items = n_tiles * n_f
    item_ids = jnp.arange(n_items, dtype=jnp.int32)
    item_end = tile_end * n_f
    item_valid = (item_ids < used * n_f).astype(jnp.int32)
    item_e = jnp.sum((item_ids[:, None] >= item_end[None, :]).astype(jnp.int32), axis=1)
    item_e = jnp.where(item_valid == 1, item_e, last_e)
    local = item_ids - take(tile_start * n_f, item_e)
    nte = jnp.maximum(take(n_tiles_e, item_e), 1)
    item_f = jnp.where(item_valid == 1, local // nte, n_f - 1).astype(jnp.int32)
    item_m = jnp.where(item_valid == 1, take(tile_start, item_e) + local % nte, used - 1).astype(jnp.int32)
    item_first = ((local % nte == 0) & (item_valid == 1)).astype(jnp.int32)
    return (pos.astype(jnp.int32), (tile_e, tile_first, tile_valid, tile_src),
            (item_e, item_f, item_m, item_first, item_valid))


def _hier_moe(h, g, w_cat, b_cat, w_gate, w_up, w_down, layer, tm=256, tf=512):
    t, d = h.shape
    n_experts = w_gate.shape[1]
    ff = w_gate.shape[-1]
    tm = min(tm, t)
    tf = min(tf, ff)
    rows = 2 * t + n_experts * tm
    route, counts = _router(h, g, w_cat, b_cat)
    pos, tiles, items = _moe_plan(route, counts, n_experts, tm, ff // tf)
    xs = _dispatch(h, g, pos, rows, tm)
    hmid = _expert_up(xs, w_gate, w_up, layer, items, tm, tf)
    ys = _expert_down(hmid, w_down, layer, tiles, tm)
    return _combine(h, route, ys, pos, tm)


def kernel(x, ssm_w_in, ssm_a_re, ssm_a_im, ssm_log_dt, ssm_b_re, ssm_b_im, ssm_c_re, ssm_c_im, ssm_d, ssm_w_glu_a, ssm_w_glu_b, kv_norm_g, w_k, w_v, attn_w_q, attn_lambda_q1, attn_lambda_k1, attn_lambda_q2, attn_lambda_k2, attn_subln_g, attn_w_o, moe_w_group, moe_b_group, moe_w_router, moe_b_router, moe_w_gate, moe_w_up, moe_w_down, norm_mix_g, norm_ffn_g, final_norm_g):
    bsz, seq, d = x.shape
    t = bsz * seq
    depth = norm_mix_g.shape[0]
    n_a = ssm_w_in.shape[0]
    head_dim = attn_lambda_q1.shape[1]
    n_heads = attn_w_q.shape[-1] // (2 * head_dim)
    n_experts = moe_w_router.shape[-1]
    h = x.reshape(t, d)

    def moe(h, li):
        pad = ROUTE_LANES - N_GROUPS - n_experts
        w_cat = jnp.pad(jnp.concatenate([moe_w_group[li], moe_w_router[li]], axis=1), ((0, 0), (0, pad)))
        b_cat = jnp.pad(jnp.concatenate([moe_b_group[li], moe_b_router[li]]), (0, pad)).reshape(1, ROUTE_LANES)
        return _hier_moe(h, norm_ffn_g[li], w_cat, b_cat, moe_w_gate, moe_w_up, moe_w_down, li)

    k_shared = v_shared = None
    for li in range(depth):
        hn = _rmsnorm(h, norm_mix_g[li], BF16)
        if li < n_a:
            a = li
            tabs = _ssm_tables(ssm_a_re[a], ssm_a_im[a], ssm_log_dt[a], ssm_b_re[a], ssm_b_im[a],
                               ssm_c_re[a], ssm_c_im[a], ssm_d[a], SSM_CHUNK)
            u = _matmul(hn, ssm_w_in[a], F32)
            y = _s5_core(u, bsz, seq, tabs, SSM_CHUNK)
            h = _glu_res(y, ssm_w_glu_a[a], ssm_w_glu_b[a], h)
        else:
            b = li - n_a
            lambda_init = 0.8 - 0.6 * math.exp(-0.3 * li)
            lam = (jnp.exp(jnp.sum(attn_lambda_q1[b] * attn_lambda_k1[b]))
                   - jnp.exp(jnp.sum(attn_lambda_q2[b] * attn_lambda_k2[b])) + lambda_init).reshape(1)
            slopes = 2.0 ** (-8.0 * jnp.arange(1, n_heads + 1, dtype=F32) / n_heads)
            q = _matmul(hn, attn_w_q[b], BF16, scale=head_dim ** -0.5)
            o = _diff_attention(q, k_shared, v_shared, lam, slopes, attn_subln_g[b], bsz, seq, n_heads,
                                1.0 - lambda_init)
            h = _matmul_res(o, attn_w_o[b], h)
        h = moe(h, li)
        if li == n_a - 1:
            kvn = _rmsnorm(h, kv_norm_g, BF16)
            k_shared = _matmul(kvn, w_k, BF16)
            v_shared = _matmul(kvn, w_v, BF16)
    return _rmsnorm(h, final_norm_g, F32).reshape(bsz, seq, d)
```

```python
import functools
import math

import jax
import jax.numpy as jnp
from jax import lax
from jax.experimental import pallas as pl
from jax.experimental.pallas import tpu as pltpu

EPS = 1e-6
F32 = jnp.float32
BF16 = jnp.bfloat16

V7X_VMEM_BYTES = 64 * 1024 * 1024
VMEM_LIMIT = V7X_VMEM_BYTES - 8 * 1024 * 1024
LANES = 128

SSM_CHUNK = 16
N_GROUPS = 4
EXPERTS_PER_GROUP = 4
ROUTE_LANES = LANES


def _cparams(sem, vmem=VMEM_LIMIT):
    return pltpu.CompilerParams(dimension_semantics=sem, vmem_limit_bytes=vmem)


def _rms(x, g):
    inv = lax.rsqrt(jnp.mean(x * x, axis=-1, keepdims=True) + EPS)
    return x * inv * g


def _rmsnorm_kernel(x_ref, g_ref, o_ref):
    o_ref[...] = _rms(x_ref[...], g_ref[...]).astype(o_ref.dtype)


def _rmsnorm(x, g, out_dtype, tm=256):
    t, d = x.shape
    tm = min(tm, t)
    return pl.pallas_call(
        _rmsnorm_kernel,
        grid=(t // tm,),
        in_specs=[pl.BlockSpec((tm, d), lambda i: (i, 0)), pl.BlockSpec((1, d), lambda i: (0, 0))],
        out_specs=pl.BlockSpec((tm, d), lambda i: (i, 0)),
        out_shape=jax.ShapeDtypeStruct((t, d), out_dtype),
        compiler_params=_cparams(("arbitrary",)),
        name="rmsnorm",
    )(x, g.reshape(1, d))


def _mm_kernel(x_ref, w_ref, o_ref, wb_ref, *, scale):
    @pl.when(pl.program_id(1) == 0)
    def _():
        wb_ref[...] = w_ref[...].astype(BF16)

    acc = jnp.dot(x_ref[...], wb_ref[...], preferred_element_type=F32)
    if scale != 1.0:
        acc = acc * scale
    o_ref[...] = acc.astype(o_ref.dtype)


def _mm_res_kernel(x_ref, w_ref, r_ref, o_ref, wb_ref):
    @pl.when(pl.program_id(1) == 0)
    def _():
        wb_ref[...] = w_ref[...].astype(BF16)

    o_ref[...] = r_ref[...] + jnp.dot(x_ref[...], wb_ref[...], preferred_element_type=F32)


def _glu_res_kernel(x_ref, wa_ref, wb_ref, r_ref, o_ref, wab_ref, wbb_ref):
    @pl.when(pl.program_id(1) == 0)
    def _():
        wab_ref[...] = wa_ref[...].astype(BF16)
        wbb_ref[...] = wb_ref[...].astype(BF16)

    x = x_ref[...].astype(BF16)
    a = jnp.dot(x, wab_ref[...], preferred_element_type=F32)
    b = jnp.dot(x, wbb_ref[...], preferred_element_type=F32)
    o_ref[...] = r_ref[...] + a * jax.nn.sigmoid(b)


def _mm_tiles(m, n, tm, tn):
    return min(tm, m), min(tn, n)


def _matmul(x, w, out_dtype, scale=1.0, tm=1024, tn=512):
    m, k = x.shape
    n = w.shape[1]
    tm, tn = _mm_tiles(m, n, tm, tn)
    return pl.pallas_call(
        functools.partial(_mm_kernel, scale=scale),
        grid=(n // tn, m // tm),
        in_specs=[pl.BlockSpec((tm, k), lambda j, i: (i, 0)), pl.BlockSpec((k, tn), lambda j, i: (0, j))],
        out_specs=pl.BlockSpec((tm, tn), lambda j, i: (i, j)),
        out_shape=jax.ShapeDtypeStruct((m, n), out_dtype),
        scratch_shapes=[pltpu.VMEM((k, tn), BF16)],
        compiler_params=_cparams(("arbitrary", "arbitrary")),
        name="matmul",
    )(x, w)


def _matmul_res(x, w, res, tm=1024, tn=512):
    m, k = x.shape
    n = w.shape[1]
    tm, tn = _mm_tiles(m, n, tm, tn)
    return pl.pallas_call(
        _mm_res_kernel,
        grid=(n // tn, m // tm),
        in_specs=[pl.BlockSpec((tm, k), lambda j, i: (i, 0)), pl.BlockSpec((k, tn), lambda j, i: (0, j)),
                  pl.BlockSpec((tm, tn), lambda j, i: (i, j))],
        out_specs=pl.BlockSpec((tm, tn), lambda j, i: (i, j)),
        out_shape=jax.ShapeDtypeStruct((m, n), F32),
        scratch_shapes=[pltpu.VMEM((k, tn), BF16)],
        compiler_params=_cparams(("arbitrary", "arbitrary")),
        name="matmul_res",
    )(x, w, res)


def _glu_res(x, wa, wb, res, tm=1024, tn=512):
    m, k = x.shape
    n = wa.shape[1]
    tm, tn = _mm_tiles(m, n, tm, tn)
    wspec = pl.BlockSpec((k, tn), lambda j, i: (0, j))
    return pl.pallas_call(
        _glu_res_kernel,
        grid=(n // tn, m // tm),
        in_specs=[pl.BlockSpec((tm, k), lambda j, i: (i, 0)), wspec, wspec,
                  pl.BlockSpec((tm, tn), lambda j, i: (i, j))],
        out_specs=pl.BlockSpec((tm, tn), lambda j, i: (i, j)),
        out_shape=jax.ShapeDtypeStruct((m, n), F32),
        scratch_shapes=[pltpu.VMEM((k, tn), BF16), pltpu.VMEM((k, tn), BF16)],
        compiler_params=_cparams(("arbitrary", "arbitrary")),
        name="glu_res",
    )(x, wa, wb, res)


def _ssm_tables(a_re, a_im, log_dt, b_re, b_im, c_re, c_im, d_skip, c):
    g_n, p_n = a_re.shape
    j_n = d_skip.shape[1]
    dt = jnp.exp(log_dt)[:, None]
    steps = jnp.arange(c + 1, dtype=F32)[:, None, None]
    mag = jnp.exp(steps * (a_re * dt)[None])
    ang = steps * (a_im * dt)[None]
    pw_re = mag * jnp.cos(ang)
    pw_im = mag * jnp.sin(ang)
    lb_re, lb_im = pw_re[1], pw_im[1]
    den = a_re * a_re + a_im * a_im
    nr = lb_re - 1.0
    q_re = (nr * a_re + lb_im * a_im) / den
    q_im = (lb_im * a_re - nr * a_im) / den
    bb_re = q_re[..., None] * b_re - q_im[..., None] * b_im
    bb_im = q_re[..., None] * b_im + q_im[..., None] * b_re
    gpb = LANES // j_n
    nb = g_n // gpb
    ct_re, ct_im = c_re.transpose(2, 0, 1), c_im.transpose(2, 0, 1)
    pt_re, pt_im = pw_re[:c].transpose(0, 2, 1)[..., None], pw_im[:c].transpose(0, 2, 1)[..., None]
    cpt_re = ct_re[None] * pt_re - ct_im[None] * pt_im
    cpt_im = ct_re[None] * pt_im + ct_im[None] * pt_re
    bk_re = bb_re.transpose(1, 2, 0)[None, :, :, :, None]
    bk_im = bb_im.transpose(1, 2, 0)[None, :, :, :, None]
    kern = jnp.sum(cpt_re[:, :, None] * bk_re - cpt_im[:, :, None] * bk_im, axis=1)
    kern = kern.at[0].add(d_skip[None] * jnp.eye(j_n, dtype=F32)[:, None, :])
    kern = kern.reshape(c, j_n, nb, gpb * j_n).transpose(2, 0, 1, 3)
    rp_re = pw_re[:c][::-1].transpose(1, 0, 2)[:, :, None, :]
    rp_im = pw_im[:c][::-1].transpose(1, 0, 2)[:, :, None, :]
    bbt_re = bb_re.transpose(0, 2, 1)[:, None]
    bbt_im = bb_im.transpose(0, 2, 1)[:, None]
    c_in = jnp.stack([rp_re * bbt_re - rp_im * bbt_im, rp_re * bbt_im + rp_im * bbt_re], axis=-2)
    c_in = c_in.reshape(nb, gpb, c, j_n, 2 * p_n).transpose(0, 2, 1, 3, 4)
    pn_re, pn_im = pw_re[1:, :, None, :], pw_im[1:, :, None, :]
    c_out = jnp.stack([c_re[None] * pn_re - c_im[None] * pn_im, -(c_re[None] * pn_im + c_im[None] * pn_re)], axis=-2)
    c_out = c_out.reshape(c, nb, gpb, j_n, 2 * p_n).transpose(1, 0, 2, 3, 4)
    r1 = jnp.arange(1, 9, dtype=F32)[:, None, None] * float(c)
    mag8 = jnp.exp(r1 * (a_re * dt)[None])
    ang8 = r1 * (a_im * dt)[None]
    l_re, l_im = mag8 * jnp.cos(ang8), mag8 * jnp.sin(ang8)
    lam_a = jnp.stack([l_re, l_re], axis=2).reshape(8, nb, gpb * 2 * p_n).transpose(1, 0, 2)
    lam_b = jnp.stack([-l_im, l_im], axis=2).reshape(8, nb, gpb * 2 * p_n).transpose(1, 0, 2)
    return dict(kern=kern, c_in=c_in.astype(BF16), c_out=c_out.astype(BF16), lam_a=lam_a, lam_b=lam_b)


def _chunk_rows(u_ref, c):
    m = u_ref.shape[0] // c
    return jnp.concatenate([u_ref[pl.ds(i, m, stride=c), :].astype(BF16) for i in range(c)], axis=-1)


def _expand_block_diag(dst_ref, src_ref):
    _, c, gpb, j_n, lw = src_ref.shape
    for i in range(c):
        for g in range(gpb):
            dst_ref[i * lw + g * j_n:i * lw + (g + 1) * j_n, g * lw:(g + 1) * lw] = src_ref[0, i, g]


def _ssm_in_kernel(u_ref, cin_ref, x_ref, win_ref, *, c):
    @pl.when(pl.program_id(0) == 0)
    def _():
        win_ref[...] = jnp.zeros_like(win_ref)

    _expand_block_diag(win_ref, cin_ref)
    x_ref[...] = jnp.dot(_chunk_rows(u_ref, c), win_ref[...], preferred_element_type=F32)


def _swap_halves(z):
    ax = z.ndim - 1
    if z.shape[ax] == LANES:
        return pltpu.roll(z, LANES // 2, axis=ax)
    parts = [pltpu.roll(z[..., v * LANES:(v + 1) * LANES], LANES // 2, axis=ax) for v in range(z.shape[ax] // LANES)]
    return jnp.concatenate(parts, axis=ax)


def _ssm_scan_kernel(x_ref, la_ref, lb_ref, s_ref, loc_ref, *, bsz, n_chunks):
    sw = x_ref.shape[1]
    ng = n_chunks // 8
    sub = lax.broadcasted_iota(jnp.int32, (1, 8, LANES), 1)

    def shifted(z, s):
        return jnp.where(sub >= s, pltpu.roll(z, s, axis=1), 0.0)

    for v in range(sw // LANES):
        cols = slice(v * LANES, (v + 1) * LANES)
        x = x_ref[:, cols].reshape(bsz * ng, 8, LANES)
        for s in (1, 2, 4):
            y = shifted(x, s)
            x = (x + la_ref[0, s - 1:s, cols].reshape(1, 1, LANES) * y
                 + lb_ref[0, s - 1:s, cols].reshape(1, 1, LANES) * _swap_halves(y))
        loc_ref[:, cols] = x.reshape(bsz * n_chunks, LANES)
        s_ref[:, cols] = shifted(x, 1).reshape(bsz * n_chunks, LANES)

    sub2 = lax.broadcasted_iota(jnp.int32, (8, sw), 0)
    p0a = jnp.where(sub2 >= 1, pltpu.roll(la_ref[0], 1, axis=0), 1.0)
    p0b = jnp.where(sub2 >= 1, pltpu.roll(lb_ref[0], 1, axis=0), 0.0)
    l8a = la_ref[0, 7:8, :]
    l8b = lb_ref[0, 7:8, :]

    def group(k, carry):
        new = []
        for b in range(bsz):
            cz = carry[b]
            cs = _swap_halves(cz)
            row0 = pl.multiple_of((b * ng + k) * 8, 8)
            s_ref[pl.ds(row0, 8), :] = s_ref[pl.ds(row0, 8), :] + p0a * cz + p0b * cs
            new.append(loc_ref[pl.ds(row0 + 7, 1), :] + l8a * cz + l8b * cs)
        return tuple(new)

    lax.fori_loop(0, ng, group, (jnp.zeros((1, sw), F32),) * bsz)


def _ssm_out_kernel(u_ref, kern_ref, s_ref, cout_ref, y_ref, w_ref, wt_ref, *, c):
    m = u_ref.shape[0] // c
    lw = u_ref.shape[1]
    j_n = kern_ref.shape[2]

    @pl.when(pl.program_id(0) == 0)
    def _():
        w_ref[...] = jnp.zeros_like(w_ref)
        wt_ref[...] = jnp.zeros_like(wt_ref)

    same_group = (lax.broadcasted_iota(jnp.int32, (lw, lw), 0) // j_n
                  == lax.broadcasted_iota(jnp.int32, (lw, lw), 1) // j_n)
    for d in range(c):
        lag = jnp.where(same_group, jnp.concatenate([kern_ref[0, d]] * (lw // j_n), axis=0), 0.0).astype(BF16)
        for ip in range(c - d):
            w_ref[ip * lw:(ip + 1) * lw, (ip + d) * lw:(ip + d + 1) * lw] = lag
    _expand_block_diag(wt_ref, cout_ref)
    y = (jnp.dot(_chunk_rows(u_ref, c), w_ref[...], preferred_element_type=F32)
         + lax.dot_general(s_ref[...].astype(BF16), wt_ref[...], (((1,), (1,)), ((), ())),
                           preferred_element_type=F32))
    y = jax.nn.gelu(y)
    for i in range(c):
        y_ref[pl.ds(i, m, stride=c), :] = y[:, i * lw:(i + 1) * lw]


def _s5_core(u, bsz, seq, tabs, c):
    t, width = u.shape
    nb, _, gpb, j_n, lw = tabs["c_in"].shape
    sw = gpb * lw
    n_chunks = seq // c
    m = bsz * n_chunks
    u_spec = pl.BlockSpec((t, lw), lambda b: (0, b))
    x_spec = pl.BlockSpec((m, sw), lambda b: (0, b))
    x_shape = jax.ShapeDtypeStruct((m, nb * sw), F32)
    c_spec = pl.BlockSpec((1, c, gpb, j_n, lw), lambda b: (b, 0, 0, 0, 0))
    l_spec = pl.BlockSpec((1, 8, sw), lambda b: (b, 0, 0))
    x = pl.pallas_call(
        functools.partial(_ssm_in_kernel, c=c),
        grid=(nb,),
        in_specs=[u_spec, c_spec],
        out_specs=x_spec,
        out_shape=x_shape,
        scratch_shapes=[pltpu.VMEM((c * lw, sw), BF16)],
        compiler_params=_cparams(("arbitrary",)),
        name="ssm_chunk_in",
    )(u, tabs["c_in"])
    s = pl.pallas_call(
        functools.partial(_ssm_scan_kernel, bsz=bsz, n_chunks=n_chunks),
        grid=(nb,),
        in_specs=[x_spec, l_spec, l_spec],
        out_specs=x_spec,
        out_shape=x_shape,
        scratch_shapes=[pltpu.VMEM((m, sw), F32)],
        compiler_params=_cparams(("arbitrary",)),
        name="ssm_chunk_scan",
    )(x, tabs["lam_a"], tabs["lam_b"])
    return pl.pallas_call(
        functools.partial(_ssm_out_kernel, c=c),
        grid=(nb,),
        in_specs=[u_spec, pl.BlockSpec((1, c, j_n, lw), lambda b: (b, 0, 0, 0)), x_spec, c_spec],
        out_specs=u_spec,
        out_shape=jax.ShapeDtypeStruct((t, width), F32),
        scratch_shapes=[pltpu.VMEM((c * lw, c * lw), BF16), pltpu.VMEM((c * lw, sw), BF16)],
        compiler_params=_cparams(("arbitrary",)),
        name="ssm_chunk_out",
    )(u, tabs["kern"], s, tabs["c_out"])


def _attn_kernel(lam_ref, slope_ref, q_ref, k_ref, v_ref, g_ref, o_ref, acc0_ref, acc1_ref, *, tq, post_scale):
    h = pl.program_id(1)
    qi = pl.program_id(2)
    slope = slope_ref[h]
    lam = lam_ref[0]
    dh = q_ref.shape[1] // 2
    q0 = q_ref[:, :dh]
    q1 = q_ref[:, dh:]
    col = lax.broadcasted_iota(jnp.int32, (1, tq), 1).astype(F32)
    nt = (((1,), (1,)), ((), ()))

    def scores(j):
        ks = pl.multiple_of(j * tq, tq)
        kt = k_ref[pl.ds(ks, tq), :]
        vt = v_ref[pl.ds(ks, tq), :]
        kb = slope * (col + ((j - qi) * tq).astype(F32))
        s0 = lax.dot_general(q0, kt[:, :dh], nt, preferred_element_type=F32) + kb
        s1 = lax.dot_general(q1, kt[:, dh:], nt, preferred_element_type=F32) + kb
        return s0, s1, vt

    def update(s, m, l, acc_ref, vt):
        m_new = jnp.maximum(m, jnp.max(s, axis=-1, keepdims=True))
        alpha = jnp.exp(m - m_new)
        p = jnp.exp(s - m_new)
        l_new = alpha * l + jnp.sum(p, axis=-1, keepdims=True)
        acc_ref[...] = alpha * acc_ref[...] + jnp.dot(p.astype(BF16), vt, preferred_element_type=F32)
        return m_new, l_new

    acc0_ref[...] = jnp.zeros_like(acc0_ref)
    acc1_ref[...] = jnp.zeros_like(acc1_ref)

    def body(j, carry):
        m0, l0, m1, l1 = carry
        s0, s1, vt = scores(j)
        m0, l0 = update(s0, m0, l0, acc0_ref, vt)
        m1, l1 = update(s1, m1, l1, acc1_ref, vt)
        return m0, l0, m1, l1

    neg = jnp.full((tq, 1), -1e30, F32)
    zero = jnp.zeros((tq, 1), F32)
    m0, l0, m1, l1 = lax.fori_loop(0, qi, body, (neg, zero, neg, zero))

    s0, s1, vt = scores(qi)
    causal = lax.broadcasted_iota(jnp.int32, (tq, tq), 0) >= lax.broadcasted_iota(jnp.int32, (tq, tq), 1)
    m0, l0 = update(jnp.where(causal, s0, -jnp.inf), m0, l0, acc0_ref, vt)
    m1, l1 = update(jnp.where(causal, s1, -jnp.inf), m1, l1, acc1_ref, vt)

    o = acc0_ref[...] / l0 - lam * (acc1_ref[...] / l1)
    o_ref[...] = (_rms(o, g_ref[...]) * post_scale).astype(o_ref.dtype)


def _diff_attention(q, k, v, lam, slopes, subln_g, bsz, seq, n_heads, post_scale, tq=512):
    t, width = q.shape
    hw = width // n_heads
    tq = min(tq, seq)
    nq = seq // tq
    return pl.pallas_call(
        functools.partial(_attn_kernel, tq=tq, post_scale=post_scale),
        grid=(bsz, n_heads, nq),
        in_specs=[pl.BlockSpec(memory_space=pltpu.SMEM), pl.BlockSpec(memory_space=pltpu.SMEM),
                  pl.BlockSpec((tq, hw), lambda b, h, i: (b * nq + i, h)),
                  pl.BlockSpec((seq, hw), lambda b, h, i: (b, h)),
                  pl.BlockSpec((seq, hw), lambda b, h, i: (b, h)),
                  pl.BlockSpec((1, hw), lambda b, h, i: (0, 0))],
        out_specs=pl.BlockSpec((tq, hw), lambda b, h, i: (b * nq + i, h)),
        out_shape=jax.ShapeDtypeStruct((t, width), BF16),
        scratch_shapes=[pltpu.VMEM((tq, hw), F32), pltpu.VMEM((tq, hw), F32)],
        compiler_params=_cparams(("arbitrary", "arbitrary", "arbitrary")),
        name="diff_attention",
    )(lam, slopes, q, k, v, subln_g.reshape(1, hw))


def _split_bf16(x):
    hi = x.astype(BF16)
    lo = (x - hi.astype(F32)).astype(BF16)
    return hi, lo


def _router_kernel(x_ref, g_ref, w_ref, b_ref, o_ref, cnt_out_ref, cnt_ref, *, n_groups, per_group):
    xn = _rms(x_ref[...], g_ref[...])
    x_hi, x_lo = _split_bf16(xn)
    w_hi, w_lo = _split_bf16(w_ref[...])
    logits = (jnp.dot(x_hi, w_hi, preferred_element_type=F32) + jnp.dot(x_hi, w_lo, preferred_element_type=F32)
              + jnp.dot(x_lo, w_hi, preferred_element_type=F32)) + b_ref[...]
    lane = lax.broadcasted_iota(jnp.int32, logits.shape, 1).astype(F32)

    def first_argmax(vals, vmax):
        return jnp.min(jnp.where(vals == vmax, lane, float(ROUTE_LANES)), axis=-1, keepdims=True)

    gl = jnp.where(lane < n_groups, logits, -jnp.inf)
    g_max = jnp.max(gl, axis=-1, keepdims=True)
    g_idx = first_argmax(gl, g_max)
    g_val = 1.0 / jnp.sum(jnp.exp(gl - g_max), axis=-1, keepdims=True)
    lo_lane = n_groups + per_group * g_idx
    el = jnp.where((lane >= lo_lane) & (lane < lo_lane + per_group), logits, -jnp.inf)
    v1 = jnp.max(el, axis=-1, keepdims=True)
    i1 = first_argmax(el, v1)
    el2 = jnp.where(lane == i1, -jnp.inf, el)
    v2 = jnp.max(el2, axis=-1, keepdims=True)
    i2 = first_argmax(el2, v2)
    e21 = jnp.exp(v2 - v1)
    w1 = g_val / (1.0 + e21)
    w2 = g_val * e21 / (1.0 + e21)
    e1 = i1 - n_groups
    e2 = i2 - n_groups
    @pl.when(pl.program_id(0) == 0)
    def _():
        cnt_ref[...] = jnp.zeros_like(cnt_ref)

    tm = logits.shape[0]
    hit1 = lane == e1
    hit2 = lane == e2
    onehot = jnp.where(hit1 | hit2, 1.0, 0.0)
    tri = jnp.where(lax.broadcasted_iota(jnp.int32, (tm, tm), 0) > lax.broadcasted_iota(jnp.int32, (tm, tm), 1),
                    1.0, 0.0).astype(BF16)
    before = jnp.dot(tri, onehot.astype(BF16), preferred_element_type=F32) + cnt_ref[0:1, :]
    rank1 = jnp.sum(jnp.where(hit1, before, 0.0), axis=-1, keepdims=True)
    rank2 = jnp.sum(jnp.where(hit2, before, 0.0), axis=-1, keepdims=True)
    cnt_ref[...] = cnt_ref[...] + jnp.sum(onehot, axis=0, keepdims=True)
    cnt_out_ref[...] = cnt_ref[...]
    out = jnp.where(lane == 0, e1, jnp.where(lane == 1, e2, jnp.where(lane == 2, w1, jnp.where(lane == 3, w2,
                    jnp.where(lane == 4, rank1, jnp.where(lane == 5, rank2, 0.0))))))
    o_ref[...] = out


def _router(h, g, w_cat, b_cat, tm=256):
    t, d = h.shape
    tm = min(tm, t)
    return pl.pallas_call(
        functools.partial(_router_kernel, n_groups=N_GROUPS, per_group=EXPERTS_PER_GROUP),
        grid=(t // tm,),
        in_specs=[pl.BlockSpec((tm, d), lambda i: (i, 0)), pl.BlockSpec((1, d), lambda i: (0, 0)),
                  pl.BlockSpec((d, ROUTE_LANES), lambda i: (0, 0)), pl.BlockSpec((1, ROUTE_LANES), lambda i: (0, 0))],
        out_specs=[pl.BlockSpec((tm, ROUTE_LANES), lambda i: (i, 0)), pl.BlockSpec((8, ROUTE_LANES), lambda i: (0, 0))],
        out_shape=[jax.ShapeDtypeStruct((t, ROUTE_LANES), F32), jax.ShapeDtypeStruct((8, ROUTE_LANES), F32)],
        scratch_shapes=[pltpu.VMEM((8, ROUTE_LANES), F32)],
        compiler_params=_cparams(("arbitrary",)),
        name="moe_router",
    )(h, g.reshape(1, d), w_cat, b_cat)


def _row_copy(src_ref, src_row, dst_ref, dst_row, sem):
    return pltpu.make_async_copy(src_ref.at[pl.ds(src_row, 1)], dst_ref.at[pl.ds(dst_row, 1)], sem)


def _dispatch_kernel(pos_ref, x_ref, g_ref, zeros_ref, xs_ref, buf_ref, sem, *, tm):
    del zeros_ref
    base = pl.program_id(0) * (2 * tm)
    buf_ref[...] = _rms(x_ref[...], g_ref[...])

    def start(r, c):
        _row_copy(buf_ref, r, xs_ref, pos_ref[base + 2 * r], sem).start()
        _row_copy(buf_ref, r, xs_ref, pos_ref[base + 2 * r + 1], sem).start()
        return c

    lax.fori_loop(0, tm, start, 0)

    def wait(r, c):
        _row_copy(buf_ref, 0, xs_ref, 0, sem).wait()
        _row_copy(buf_ref, 0, xs_ref, 0, sem).wait()
        return c

    lax.fori_loop(0, tm, wait, 0)


def _dispatch(h, g, pos, rows, tm=256):
    t, d = h.shape
    tm = min(tm, t)
    return pl.pallas_call(
        functools.partial(_dispatch_kernel, tm=tm),
        grid_spec=pltpu.PrefetchScalarGridSpec(
            num_scalar_prefetch=1,
            grid=(t // tm,),
            in_specs=[pl.BlockSpec((tm, d), lambda i, pos: (i, 0)), pl.BlockSpec((1, d), lambda i, pos: (0, 0)),
                      pl.BlockSpec(memory_space=pl.ANY)],
            out_specs=pl.BlockSpec(memory_space=pl.ANY),
            scratch_shapes=[pltpu.VMEM((tm, d), F32), pltpu.SemaphoreType.DMA(())],
        ),
        out_shape=jax.ShapeDtypeStruct((rows, d), F32),
        input_output_aliases={3: 0},
        compiler_params=_cparams(("arbitrary",)),
        name="moe_dispatch",
    )(pos, h, g.reshape(1, d), jnp.zeros((rows, d), F32))


def _expert_up_kernel(ie_ref, if_ref, im_ref, ifirst_ref, ivalid_ref, xs_ref, wg_ref, wu_ref, zeros_ref, o_ref,
                      wgb_ref, wub_ref):
    del ie_ref, if_ref, im_ref, zeros_ref
    i = pl.program_id(0)

    @pl.when(ifirst_ref[i] == 1)
    def _():
        wgb_ref[...] = wg_ref[...].astype(BF16)
        wub_ref[...] = wu_ref[...].astype(BF16)

    @pl.when(ivalid_ref[i] == 1)
    def _():
        x = xs_ref[...].astype(BF16)
        gate = jnp.dot(x, wgb_ref[...], preferred_element_type=F32)
        up = jnp.dot(x, wub_ref[...], preferred_element_type=F32)
        o_ref[...] = (jax.nn.silu(gate) * up).astype(o_ref.dtype)


def _expert_up(xs, w_gate, w_up, layer, items, tm, tf):
    rows, d = xs.shape
    ff = w_gate.shape[-1]
    n_items = items[0].shape[0]
    wspec = pl.BlockSpec((None, None, d, tf), lambda i, ie, jf, im, ifst, iv: (layer, ie[i], 0, jf[i]))
    return pl.pallas_call(
        _expert_up_kernel,
        grid_spec=pltpu.PrefetchScalarGridSpec(
            num_scalar_prefetch=5,
            grid=(n_items,),
            in_specs=[pl.BlockSpec((tm, d), lambda i, ie, jf, im, ifst, iv: (im[i], 0)), wspec, wspec,
                      pl.BlockSpec(memory_space=pl.ANY)],
            out_specs=pl.BlockSpec((tm, tf), lambda i, ie, jf, im, ifst, iv: (im[i], jf[i])),
            scratch_shapes=[pltpu.VMEM((d, tf), BF16), pltpu.VMEM((d, tf), BF16)],
        ),
        out_shape=jax.ShapeDtypeStruct((rows, ff), BF16),
        input_output_aliases={8: 0},
        compiler_params=_cparams(("arbitrary",)),
        name="moe_expert_up",
    )(*items, xs, w_gate, w_up, jnp.zeros((rows, ff), BF16))


def _expert_down_kernel(te_ref, tfirst_ref, tvalid_ref, tsrc_ref, h_ref, wd_ref, o_ref, wdb_ref):
    del te_ref, tsrc_ref
    i = pl.program_id(0)

    @pl.when(tfirst_ref[i] == 1)
    def _():
        wdb_ref[...] = wd_ref[...].astype(BF16)

    @pl.when(tvalid_ref[i] == 1)
    def _():
        o_ref[...] = jnp.dot(h_ref[...], wdb_ref[...], preferred_element_type=F32)

    @pl.when(tvalid_ref[i] == 0)
    def _():
        o_ref[...] = jnp.zeros_like(o_ref)


def _expert_down(hmid, w_down, layer, tiles, tm):
    rows, ff = hmid.shape
    d = w_down.shape[-1]
    n_tiles = rows // tm
    return pl.pallas_call(
        _expert_down_kernel,
        grid_spec=pltpu.PrefetchScalarGridSpec(
            num_scalar_prefetch=4,
            grid=(n_tiles,),
            in_specs=[pl.BlockSpec((tm, ff), lambda i, te, tf, tv, ts: (ts[i], 0)),
                      pl.BlockSpec((None, None, ff, d), lambda i, te, tf, tv, ts: (layer, te[i], 0, 0))],
            out_specs=pl.BlockSpec((tm, d), lambda i, te, tf, tv, ts: (i, 0)),
            scratch_shapes=[pltpu.VMEM((ff, d), BF16)],
        ),
        out_shape=jax.ShapeDtypeStruct((rows, d), F32),
        compiler_params=_cparams(("arbitrary",)),
        name="moe_expert_down",
    )(*tiles, hmid, w_down)


def _combine_kernel(pos_ref, h_ref, route_ref, ys_ref, g_refs, out_refs, buf_ref, sem, *, tm, keep_h):
    base = pl.program_id(0) * (2 * tm)

    def start(r, c):
        _row_copy(ys_ref, pos_ref[base + 2 * r], buf_ref.at[0], r, sem).start()
        _row_copy(ys_ref, pos_ref[base + 2 * r + 1], buf_ref.at[1], r, sem).start()
        return c

    lax.fori_loop(0, tm, start, 0)

    def wait(r, c):
        _row_copy(ys_ref, 0, buf_ref.at[0], 0, sem).wait()
        _row_copy(ys_ref, 0, buf_ref.at[1], 0, sem).wait()
        return c

    lax.fori_loop(0, tm, wait, 0)
    route = route_ref[...]
    h_new = h_ref[...] + route[:, 2:3] * buf_ref[0] + route[:, 3:4] * buf_ref[1]
    outs = list(out_refs)
    if keep_h:
        outs.pop(0)[...] = h_new
    for g_ref, n_ref in zip(g_refs, outs):
        n_ref[...] = _rms(h_new, g_ref[...]).astype(n_ref.dtype)


def _combine(h, route, ys, pos, norms, keep_h, tm=256):
    t, d = h.shape
    tm = min(tm, t)
    n_norm = len(norms)

    def body(pos_ref, h_ref, route_ref, ys_ref, *rest):
        g_refs, out_refs, (buf_ref, sem) = rest[:n_norm], rest[n_norm:-2], rest[-2:]
        _combine_kernel(pos_ref, h_ref, route_ref, ys_ref, g_refs, out_refs, buf_ref, sem, tm=tm, keep_h=keep_h)

    row_spec = pl.BlockSpec((tm, d), lambda i, pos: (i, 0))
    out_dtypes = ([F32] if keep_h else []) + [dt for _, dt in norms]
    return pl.pallas_call(
        body,
        grid_spec=pltpu.PrefetchScalarGridSpec(
            num_scalar_prefetch=1,
            grid=(t // tm,),
            in_specs=[row_spec, pl.BlockSpec((tm, ROUTE_LANES), lambda i, pos: (i, 0)),
                      pl.BlockSpec(memory_space=pl.ANY)] + [pl.BlockSpec((1, d), lambda i, pos: (0, 0))] * n_norm,
            out_specs=[row_spec] * len(out_dtypes),
            scratch_shapes=[pltpu.VMEM((2, tm, d), F32), pltpu.SemaphoreType.DMA(())],
        ),
        out_shape=[jax.ShapeDtypeStruct((t, d), dt) for dt in out_dtypes],
        compiler_params=_cparams(("arbitrary",)),
        name="moe_combine",
    )(pos, h, route, ys, *[g.reshape(1, d) for g, _ in norms])


def _moe_plan(route, counts, n_experts, tm, n_f):
    t = route.shape[0]
    experts = jnp.arange(n_experts, dtype=jnp.int32)

    def take(table, idx):
        return jnp.sum(jnp.where(idx[:, None] == experts[None, :], table[None, :], 0), axis=1)

    eid = route[:, :2].astype(jnp.int32).reshape(-1)
    rank = route[:, 4:6].astype(jnp.int32).reshape(-1)
    counts = counts[0, :n_experts].astype(jnp.int32)
    n_tiles_e = (counts + tm - 1) // tm
    tile_end = jnp.cumsum(n_tiles_e)
    tile_start = tile_end - n_tiles_e
    pos = take(tile_start * tm, eid) + rank
    n_tiles = (2 * t) // tm + n_experts
    used = tile_end[-1]
    tile_ids = jnp.arange(n_tiles, dtype=jnp.int32)
    tile_valid = (tile_ids < used).astype(jnp.int32)
    last_e = jnp.max(jnp.where(n_tiles_e > 0, experts, 0))
    tile_e = jnp.sum((tile_ids[:, None] >= tile_end[None, :]).astype(jnp.int32), axis=1)
    tile_e = jnp.where(tile_valid == 1, tile_e, last_e)
    tile_first = ((tile_ids == take(tile_start, tile_e)) & (tile_valid == 1)).astype(jnp.int32)
    tile_src = jnp.minimum(tile_ids, used - 1).astype(jnp.int32)
    n_items = n_tiles * n_f
    item_ids = jnp.arange(n_items, dtype=jnp.int32)
    item_end = tile_end * n_f
    item_valid = (item_ids < used * n_f).astype(jnp.int32)
    item_e = jnp.sum((item_ids[:, None] >= item_end[None, :]).astype(jnp.int32), axis=1)
    item_e = jnp.where(item_valid == 1, item_e, last_e)
    local = item_ids - take(tile_start * n_f, item_e)
    nte = jnp.maximum(take(n_tiles_e, item_e), 1)
    item_f = jnp.where(item_valid == 1, local // nte, n_f - 1).astype(jnp.int32)
    item_m = jnp.where(item_valid == 1, take(tile_start, item_e) + local % nte, used - 1).astype(jnp.int32)
    item_first = ((local % nte == 0) & (item_valid == 1)).astype(jnp.int32)
    return (pos.astype(jnp.int32), (tile_e, tile_first, tile_valid, tile_src),
            (item_e, item_f, item_m, item_first, item_valid))


def _hier_moe(h, g, w_cat, b_cat, w_gate, w_up, w_down, layer, norms=(), keep_h=True, tm=256, tf=512):
    t, d = h.shape
    n_experts = w_gate.shape[1]
    ff = w_gate.shape[-1]
    tm = min(tm, t)
    tf = min(tf, ff)
    rows = 2 * t + n_experts * tm
    route, counts = _router(h, g, w_cat, b_cat)
    pos, tiles, items = _moe_plan(route, counts, n_experts, tm, ff // tf)
    xs = _dispatch(h, g, pos, rows, tm)
    hmid = _expert_up(xs, w_gate, w_up, layer, items, tm, tf)
    ys = _expert_down(hmid, w_down, layer, tiles, tm)
    return _combine(h, route, ys, pos, list(norms), keep_h, tm)


def kernel(x, ssm_w_in, ssm_a_re, ssm_a_im, ssm_log_dt, ssm_b_re, ssm_b_im, ssm_c_re, ssm_c_im, ssm_d, ssm_w_glu_a, ssm_w_glu_b, kv_norm_g, w_k, w_v, attn_w_q, attn_lambda_q1, attn_lambda_k1, attn_lambda_q2, attn_lambda_k2, attn_subln_g, attn_w_o, moe_w_group, moe_b_group, moe_w_router, moe_b_router, moe_w_gate, moe_w_up, moe_w_down, norm_mix_g, norm_ffn_g, final_norm_g):
    bsz, seq, d = x.shape
    t = bsz * seq
    depth = norm_mix_g.shape[0]
    n_a = ssm_w_in.shape[0]
    head_dim = attn_lambda_q1.shape[1]
    n_heads = attn_w_q.shape[-1] // (2 * head_dim)
    n_experts = moe_w_router.shape[-1]
    h = x.reshape(t, d)

    def moe(h, li, norms, keep_h):
        pad = ROUTE_LANES - N_GROUPS - n_experts
        w_cat = jnp.pad(jnp.concatenate([moe_w_group[li], moe_w_router[li]], axis=1), ((0, 0), (0, pad)))
        b_cat = jnp.pad(jnp.concatenate([moe_b_group[li], moe_b_router[li]]), (0, pad)).reshape(1, ROUTE_LANES)
        return _hier_moe(h, norm_ffn_g[li], w_cat, b_cat, moe_w_gate, moe_w_up, moe_w_down, li, norms, keep_h)

    k_shared = v_shared = None
    hn = _rmsnorm(h, norm_mix_g[0], BF16)
    for li in range(depth):
        if li < n_a:
            a = li
            tabs = _ssm_tables(ssm_a_re[a], ssm_a_im[a], ssm_log_dt[a], ssm_b_re[a], ssm_b_im[a],
                               ssm_c_re[a], ssm_c_im[a], ssm_d[a], SSM_CHUNK)
            u = _matmul(hn, ssm_w_in[a], F32)
            y = _s5_core(u, bsz, seq, tabs, SSM_CHUNK)
            h = _glu_res(y, ssm_w_glu_a[a], ssm_w_glu_b[a], h)
        else:
            b = li - n_a
            lambda_init = 0.8 - 0.6 * math.exp(-0.3 * li)
            lam = (jnp.exp(jnp.sum(attn_lambda_q1[b] * attn_lambda_k1[b]))
                   - jnp.exp(jnp.sum(attn_lambda_q2[b] * attn_lambda_k2[b])) + lambda_init).reshape(1)
            slopes = 2.0 ** (-8.0 * jnp.arange(1, n_heads + 1, dtype=F32) / n_heads)
            q = _matmul(hn, attn_w_q[b], BF16, scale=head_dim ** -0.5)
            o = _diff_attention(q, k_shared, v_shared, lam, slopes, attn_subln_g[b], bsz, seq, n_heads,
                                1.0 - lambda_init)
            h = _matmul_res(o, attn_w_o[b], h)
        last = li == depth - 1
        norms = [(final_norm_g, F32)] if last else [(norm_mix_g[li + 1], BF16)]
        if li == n_a - 1:
            norms.append((kv_norm_g, BF16))
        outs = moe(h, li, norms, not last)
        if last:
            return outs[0].reshape(bsz, seq, d)
        h, hn = outs[0], outs[1]
        if li == n_a - 1:
            k_shared = _matmul(outs[2], w_k, BF16)
            v_shared = _matmul(outs[2], w_v, BF16)
```
